```python
import math, functools
import jax, jax.numpy as jnp
from jax import lax
import numpy as np

D_MODEL = 4096
BATCH = 16
SEQ = 256
DEPTH = 4
DEC_BATCH = 4
DEC_SEQ = 4096
PAST_LEN = 256

GRID_W = 64
MIX_WIDTH = D_MODEL
POOL_WIDTH = D_MODEL // 4
N_POOL_GROUPS = 4
POOL_GROUP_DIM = POOL_WIDTH // N_POOL_GROUPS
POOL_WINDOWS = (2, 4, 8, 16)
NA_HEADS = 16
NA_HEAD_DIM = 128
NA_WIDTH = NA_HEADS * NA_HEAD_DIM
WIN_R = 8
WIN_C = 16
Q_BLOCK = 128
RG_WIDTH = D_MODEL // 4
RG_BLOCKS = 8
RG_BLOCK_DIM = RG_WIDTH // RG_BLOCKS
CONV_W = 4
RG_C = 8.0
IN_WIDTH = POOL_WIDTH + 3 * NA_WIDTH + 2 * RG_WIDTH
SPLIT_POINTS = (POOL_WIDTH, POOL_WIDTH + NA_WIDTH, POOL_WIDTH + 2 * NA_WIDTH,
                POOL_WIDTH + 3 * NA_WIDTH, POOL_WIDTH + 3 * NA_WIDTH + RG_WIDTH)
N_EXPERTS = 16
N_EXPERT_GROUPS = 4
GROUP_SIZE = N_EXPERTS // N_EXPERT_GROUPS
TOP_K = 2
D_FF = 1024
MOE_BLOCK = 256
EPS = 1e-6

kernel_name = 'hybrid_pool_natten_rglru_moe_dit_step'


def rmsnorm(x, g):
    xf = x.astype(jnp.float32)
    y = xf * lax.rsqrt(jnp.mean(xf * xf, axis=-1, keepdims=True) + EPS)
    return (y * g.astype(jnp.float32)).astype(x.dtype)


def pool_mixer(p, w_pool, pool_scale):
    B, T, _ = p.shape
    pg = p.astype(jnp.float32).reshape(B, T, N_POOL_GROUPS, POOL_GROUP_DIM)
    cs = jnp.concatenate([jnp.zeros_like(pg[:, :1]), jnp.cumsum(pg, axis=1)], axis=1)
    t = jnp.arange(T)[:, None]
    win = jnp.array(POOL_WINDOWS, jnp.int32)[None, :]
    lo = jnp.clip(t - win // 2, 0, T)
    hi = jnp.clip(t + win - win // 2, 0, T)
    gi = jnp.arange(N_POOL_GROUPS)[None, :]
    mean = (cs[:, hi, gi] - cs[:, lo, gi]) / (hi - lo).astype(jnp.float32)[None, :, :, None]
    pooled = (mean - pg).astype(p.dtype)
    out = jnp.einsum('btgc,gcd->btgd', pooled, w_pool).reshape(B, T, POOL_WIDTH)
    return out * pool_scale


def linear_scan(a, b, h0):
    def combine(l, r):
        return (l[0] * r[0], r[0] * l[1] + r[1])
    a_cum, b_cum = lax.associative_scan(combine, (a, b), axis=1)
    return b_cum + a_cum * h0[:, None, :]


def rglru_mixer(xr, gate, conv_w, conv_b, w_a, b_a, w_i, b_i, lam, h0_f, h0_b):
    B, T, W = xr.shape
    xp = jnp.pad(xr, ((0, 0), (CONV_W // 2, CONV_W - 1 - CONV_W // 2), (0, 0)))
    xc = sum(xp[:, k:k + T] * conv_w[k] for k in range(CONV_W)) + conv_b
    xh = xc.reshape(B, T, RG_BLOCKS, RG_BLOCK_DIM)
    r = jax.nn.sigmoid((jnp.einsum('btnc,dncm->dbtnm', xh, w_a).reshape(2, B, T, W)
                        + b_a[:, None, None, :]).astype(jnp.float32))
    i = jax.nn.sigmoid((jnp.einsum('btnc,dncm->dbtnm', xh, w_i).reshape(2, B, T, W)
                        + b_i[:, None, None, :]).astype(jnp.float32))
    log_a = -RG_C * r * jax.nn.softplus(-lam.astype(jnp.float32))[:, None, None, :]
    a = jnp.exp(log_a)
    bx = jnp.sqrt(-jnp.expm1(2.0 * log_a)) * i * xc.astype(jnp.float32)[None]
    hf = linear_scan(a[0], bx[0], h0_f.astype(jnp.float32))
    hb = jnp.flip(linear_scan(jnp.flip(a[1], 1), jnp.flip(bx[1], 1), h0_b.astype(jnp.float32)), 1)
    y = ((hf + hb) * jax.nn.gelu(gate.astype(jnp.float32))).astype(xr.dtype)
    return y, hf[:, -1].astype(xr.dtype), hb[:, 0].astype(xr.dtype)


def context_attention(q, k, v):
    B, S, H, Dh = q.shape
    scale = Dh ** -0.5
    qb = q.reshape(B, S // Q_BLOCK, Q_BLOCK, H, Dh).transpose(1, 0, 2, 3, 4)

    def block(qi):
        s = jnp.einsum('bqhd,bkhd->bhqk', qi, k).astype(jnp.float32) * scale
        p = jax.nn.softmax(s, axis=-1).astype(v.dtype)
        return jnp.einsum('bhqk,bkhd->bqhd', p, v)

    o = lax.map(block, qb)
    return o.transpose(1, 0, 2, 3, 4).reshape(B, S, H * Dh)


def natten_latent(q, k, v, ck, cv, rpb):
    B, T, H, Dh = q.shape
    rows = T // GRID_W
    wr = min(WIN_R, rows)
    scale = Dh ** -0.5
    qg = q.reshape(B, rows, GRID_W, H, Dh)
    kg = k.reshape(B, rows, GRID_W, H, Dh)
    vg = v.reshape(B, rows, GRID_W, H, Dh)
    row_start = jnp.clip(jnp.arange(rows) - wr // 2, 0, rows - wr)
    cols = (jnp.clip(jnp.arange(GRID_W) - WIN_C // 2, 0, GRID_W - WIN_C)[:, None]
            + jnp.arange(WIN_C)[None, :])
    col_bias_idx = cols - jnp.arange(GRID_W)[:, None] + (WIN_C - 1)
    n_loc = wr * WIN_C

    def row_block(r):
        rs = row_start[r]
        k_rows = lax.dynamic_slice_in_dim(kg, rs, wr, axis=1)
        v_rows = lax.dynamic_slice_in_dim(vg, rs, wr, axis=1)
        k_win = k_rows[:, :, cols]
        v_win = v_rows[:, :, cols]
        q_row = lax.dynamic_index_in_dim(qg, r, axis=1, keepdims=False)
        row_bias_idx = rs + jnp.arange(wr) - r + (WIN_R - 1)
        bias = rpb[:, row_bias_idx][:, :, col_bias_idx]
        s_loc = (jnp.einsum('bqhd,bwqkhd->bhqwk', q_row, k_win).astype(jnp.float32) * scale
                 + bias.transpose(0, 2, 1, 3)[None].astype(jnp.float32))
        s_ctx = jnp.einsum('bqhd,bchd->bhqc', q_row, ck).astype(jnp.float32) * scale
        s = jnp.concatenate([s_loc.reshape(B, H, GRID_W, n_loc), s_ctx], axis=-1)
        p = jax.nn.softmax(s, axis=-1).astype(v.dtype)
        p_loc = p[..., :n_loc].reshape(B, H, GRID_W, wr, WIN_C)
        return (jnp.einsum('bhqwk,bwqkhd->bqhd', p_loc, v_win)
                + jnp.einsum('bhqc,bchd->bqhd', p[..., n_loc:], cv))

    o = lax.map(row_block, jnp.arange(rows))
    return o.transpose(1, 0, 2, 3, 4).reshape(B, T, H * Dh)


def moe_ffn(u, w_router, b_router, w_up, w_down):
    shp = u.shape
    D = shp[-1]
    xf = u.reshape(-1, D)
    N = xf.shape[0]
    logits = xf.astype(jnp.float32) @ w_router.astype(jnp.float32) + b_router.astype(jnp.float32)
    top_v, top_i = lax.top_k(logits.reshape(N, N_EXPERT_GROUPS, GROUP_SIZE), TOP_K)
    g_sel = jnp.argmax(top_v.sum(-1), axis=-1)
    sel_v = jnp.take_along_axis(top_v, g_sel[:, None, None], axis=1)[:, 0]
    sel_i = jnp.take_along_axis(top_i, g_sel[:, None, None], axis=1)[:, 0]
    expert = g_sel[:, None] * GROUP_SIZE + sel_i
    gate = jax.nn.softmax(sel_v, axis=-1)
    A = N * TOP_K
    flat_e = expert.reshape(A).astype(jnp.int32)
    flat_tok = jnp.arange(A, dtype=jnp.int32) // TOP_K
    order = jnp.argsort(flat_e)
    e_sorted = flat_e[order]
    counts = jnp.bincount(flat_e, length=N_EXPERTS)
    padded = ((counts + MOE_BLOCK - 1) // MOE_BLOCK) * MOE_BLOCK
    pad_end = jnp.cumsum(padded)
    pad_start = pad_end - padded
    start = jnp.cumsum(counts) - counts
    dest = pad_start[e_sorted] + (jnp.arange(A) - start[e_sorted])
    n_blocks = (A + MOE_BLOCK - 1) // MOE_BLOCK + N_EXPERTS
    P = n_blocks * MOE_BLOCK
    slot_tok = jnp.full((P,), N, jnp.int32).at[dest].set(flat_tok[order])
    slot_gate = jnp.zeros((P,), u.dtype).at[dest].set(gate.reshape(A)[order].astype(u.dtype))
    block_e = jnp.minimum(jnp.searchsorted(pad_end, jnp.arange(n_blocks) * MOE_BLOCK, side='right'),
                          N_EXPERTS - 1)
    x_pad = jnp.concatenate([xf, jnp.zeros((1, D), xf.dtype)], axis=0)
    xs = x_pad[slot_tok].reshape(n_blocks, MOE_BLOCK, D)

    def expert_block(args):
        xb, e = args
        h_gate, h_up = jnp.split(xb @ w_up[e], 2, axis=-1)
        return (jax.nn.silu(h_gate) * h_up) @ w_down[e]

    ys = lax.map(expert_block, (xs, block_e)).reshape(P, D)
    out = jax.ops.segment_sum(ys * slot_gate[:, None], slot_tok, num_segments=N + 1)[:N]
    return out.reshape(shp)


def trunk_layer(x, cond, p, attend, h0_f, h0_b):
    B, T, _ = x.shape
    mod = jax.nn.silu(cond) @ p['w_ada'] + p['b_ada']
    sh1, sc1, g1, sh2, sc2, g2 = jnp.split(mod[:, None, :], 6, axis=-1)
    u = rmsnorm(x, p['norm_attn']) * (1 + sc1) + sh1
    pool_in, q, k, v, rx, rgate = jnp.split(u @ p['w_in'], SPLIT_POINTS, axis=-1)
    q = q.reshape(B, T, NA_HEADS, NA_HEAD_DIM)
    k = k.reshape(B, T, NA_HEADS, NA_HEAD_DIM)
    v = v.reshape(B, T, NA_HEADS, NA_HEAD_DIM)
    out_a = pool_mixer(pool_in, p['w_pool'], p['pool_scale'])
    out_b = attend(q, k, v)
    out_c, hf, hb = rglru_mixer(rx, rgate, p['conv_w'], p['conv_b'], p['rg_w_a'], p['rg_b_a'],
                                p['rg_w_i'], p['rg_b_i'], p['rg_lam'], h0_f, h0_b)
    x = x + g1 * (jnp.concatenate([out_a, out_b, out_c], axis=-1) @ p['w_out'])
    u2 = rmsnorm(x, p['norm_ffn']) * (1 + sc2) + sh2
    x = x + g2 * moe_ffn(u2, p['w_router'], p['b_router'], p['w_up'], p['w_down'])
    return x, (k, v, jnp.stack([hf, hb], axis=1))


def setup_inputs(seed: int = 0) -> dict:
    key = jax.random.key(seed)
    ks = jax.random.split(key, 32)

    def nrm(k, shape, s):
        return jax.random.normal(k, shape, jnp.float32) * s

    u_a = jax.random.uniform(ks[22], (DEPTH, 2, RG_WIDTH), jnp.float32, 0.9, 0.999)
    s_a = u_a ** (1.0 / RG_C)
    return {
        'x_prompt': nrm(ks[0], (BATCH, SEQ, D_MODEL), 1.0),
        'x_sample': nrm(ks[1], (DEC_BATCH, DEC_SEQ, D_MODEL), 1.0),
        'cache_k': nrm(ks[2], (DEC_BATCH, DEPTH, PAST_LEN, NA_HEADS, NA_HEAD_DIM), 1.0),
        'cache_v': nrm(ks[3], (DEC_BATCH, DEPTH, PAST_LEN, NA_HEADS, NA_HEAD_DIM), 1.0),
        'state_rglru': nrm(ks[4], (DEC_BATCH, DEPTH, 2, RG_WIDTH), 0.5),
        'c': nrm(ks[5], (DEC_BATCH, D_MODEL), 1.0),
        'c_ctx': nrm(ks[6], (D_MODEL,), 1.0),
        'w_ada': nrm(ks[7], (DEPTH, D_MODEL, 6 * D_MODEL), 0.5 * D_MODEL ** -0.5),
        'b_ada': nrm(ks[8], (DEPTH, 6 * D_MODEL), 0.02),
        'norm_attn': 1.0 + nrm(ks[9], (DEPTH, D_MODEL), 0.02),
        'norm_ffn': 1.0 + nrm(ks[10], (DEPTH, D_MODEL), 0.02),
        'norm_final': 1.0 + nrm(ks[11], (D_MODEL,), 0.02),
        'w_in': nrm(ks[12], (DEPTH, D_MODEL, IN_WIDTH), D_MODEL ** -0.5),
        'w_pool': nrm(ks[13], (DEPTH, N_POOL_GROUPS, POOL_GROUP_DIM, POOL_GROUP_DIM), POOL_GROUP_DIM ** -0.5),
        'pool_scale': 1.0 + nrm(ks[14], (DEPTH, POOL_WIDTH), 0.1),
        'rpb': nrm(ks[15], (DEPTH, NA_HEADS, 2 * WIN_R - 1, 2 * WIN_C - 1), 0.1),
        'conv_w': nrm(ks[16], (DEPTH, CONV_W, RG_WIDTH), CONV_W ** -0.5),
        'conv_b': nrm(ks[17], (DEPTH, RG_WIDTH), 0.02),
        'rg_w_a': nrm(ks[18], (DEPTH, 2, RG_BLOCKS, RG_BLOCK_DIM, RG_BLOCK_DIM), RG_BLOCK_DIM ** -0.5),
        'rg_b_a': nrm(ks[19], (DEPTH, 2, RG_WIDTH), 0.02),
        'rg_w_i': nrm(ks[20], (DEPTH, 2, RG_BLOCKS, RG_BLOCK_DIM, RG_BLOCK_DIM), RG_BLOCK_DIM ** -0.5),
        'rg_b_i': nrm(ks[21], (DEPTH, 2, RG_WIDTH), 0.02),
        'rg_lam': jnp.log(s_a) - jnp.log1p(-s_a),
        'w_out': nrm(ks[23], (DEPTH, MIX_WIDTH, D_MODEL), MIX_WIDTH ** -0.5),
        'w_router': nrm(ks[24], (D_MODEL, N_EXPERTS), D_MODEL ** -0.5),
        'b_router': nrm(ks[25], (N_EXPERTS,), 0.01),
        'w_up': nrm(ks[26], (DEPTH, N_EXPERTS, D_MODEL, 2 * D_FF), D_MODEL ** -0.5),
        'w_down': nrm(ks[27], (DEPTH, N_EXPERTS, D_FF, D_MODEL), D_FF ** -0.5),
    }


def reference(x_prompt, x_sample, cache_k, cache_v, state_rglru, c, c_ctx, w_ada, b_ada,
              norm_attn, norm_ffn, norm_final, w_in, w_pool, pool_scale, rpb, conv_w, conv_b,
              rg_w_a, rg_b_a, rg_w_i, rg_b_i, rg_lam, w_out, w_router, b_router, w_up, w_down):
    xp = x_prompt
    xs = x_sample
    zeros_h = jnp.zeros((x_prompt.shape[0], RG_WIDTH), x_prompt.dtype)
    new_k, new_v, new_h = [], [], []
    for l in range(DEPTH):
        p = {'w_ada': w_ada[l], 'b_ada': b_ada[l], 'norm_attn': norm_attn[l], 'norm_ffn': norm_ffn[l],
             'w_in': w_in[l], 'w_pool': w_pool[l], 'pool_scale': pool_scale[l],
             'conv_w': conv_w[l], 'conv_b': conv_b[l], 'rg_w_a': rg_w_a[l], 'rg_b_a': rg_b_a[l],
             'rg_w_i': rg_w_i[l], 'rg_b_i': rg_b_i[l], 'rg_lam': rg_lam[l], 'w_out': w_out[l],
             'w_router': w_router, 'b_router': b_router, 'w_up': w_up[l], 'w_down': w_down[l]}
        xp, (k_l, v_l, h_l) = trunk_layer(xp, c_ctx[None, :], p, context_attention, zeros_h, zeros_h)
        new_k.append(k_l)
        new_v.append(v_l)
        new_h.append(h_l)
        attend_latent = functools.partial(natten_latent, ck=cache_k[:, l], cv=cache_v[:, l], rpb=rpb[l])
        xs, _ = trunk_layer(xs, c, p, attend_latent, state_rglru[:, l, 0], state_rglru[:, l, 1])
    y_prompt = rmsnorm(xp, norm_final)
    y_sample = rmsnorm(xs, norm_final)
    new_cache_k = jnp.stack(new_k, axis=1)
    new_cache_v = jnp.stack(new_v, axis=1)
    new_state_rglru = jnp.stack(new_h, axis=1)
    return (y_prompt, y_sample, new_cache_k, new_cache_v, new_state_rglru)
```

```python
import functools

import jax
import jax.numpy as jnp
from jax import lax
from jax.experimental import pallas as pl
from jax.experimental.pallas import tpu as pltpu

F32 = jnp.float32
BF16 = jnp.bfloat16

EPS = 1e-6
GRID_W = 64
POOL_WINDOWS = (2, 4, 8, 16)
N_POOL_GROUPS = 4
WIN_R = 8
WIN_C = 16
NA_HEAD_DIM = 128
RG_BLOCK_DIM = 128
CONV_W = 4
RG_C = 8.0
N_EXPERTS = 16
N_EXPERT_GROUPS = 4
GROUP_SIZE = N_EXPERTS // N_EXPERT_GROUPS
TOP_K = 2
MOE_BLOCK = 256

NEG_BIG = -1e30
VMEM_LIMIT_BYTES = 52 * 1024 * 1024

NAT_R = 4
NAT_NK = 12


def _cparams(n_axes):
    return pltpu.CompilerParams(
        dimension_semantics=("arbitrary",) * n_axes, vmem_limit_bytes=VMEM_LIMIT_BYTES)


def _silu(x):
    return x * jax.nn.sigmoid(x)


def _ada_kernel(c_ref, w_ref, b_ref, o_ref):
    s = _silu(c_ref[...]).astype(BF16)
    w = w_ref[...].astype(BF16)
    o_ref[...] = jnp.dot(s, w, preferred_element_type=F32) + b_ref[...]


def ada_mod(cond, w_ada, b_ada, tn=512):
    L, D, N = w_ada.shape
    R = cond.shape[0]
    return pl.pallas_call(
        _ada_kernel,
        grid=(L, N // tn),
        in_specs=[
            pl.BlockSpec((R, D), lambda l, j: (0, 0)),
            pl.BlockSpec((None, D, tn), lambda l, j: (l, 0, j)),
            pl.BlockSpec((None, 1, tn), lambda l, j: (l, 0, j)),
        ],
        out_specs=pl.BlockSpec((None, R, tn), lambda l, j: (l, 0, j)),
        out_shape=jax.ShapeDtypeStruct((L, R, N), F32),
        compiler_params=_cparams(2),
        name="ada_mod",
    )(cond, w_ada, b_ada.reshape(L, 1, N))


def _rms(x, g):
    ms = jnp.mean(x * x, axis=-1, keepdims=True)
    return x * lax.rsqrt(ms + EPS) * g


def _norm_mod_kernel(x_ref, g_ref, sc_ref, sh_ref, u_ref):
    y = _rms(x_ref[...], g_ref[...])
    u_ref[...] = (y * (1.0 + sc_ref[0]) + sh_ref[0]).astype(u_ref.dtype)


def _norm_mod_router_kernel(x_ref, g_ref, sc_ref, sh_ref, wr_ref, br_ref, u_ref, lg_ref):
    y = _rms(x_ref[...], g_ref[...])
    u = y * (1.0 + sc_ref[0]) + sh_ref[0]
    u_ref[...] = u.astype(u_ref.dtype)
    lg_ref[...] = jnp.dot(u.astype(BF16), wr_ref[...], preferred_element_type=F32) + br_ref[...]


def _norm_plain_kernel(x_ref, g_ref, y_ref):
    y_ref[...] = _rms(x_ref[...], g_ref[...])


def _cond_row_fn(tm, n_ctx_rows, dec_seq):
    def row(i):
        r0 = i * tm
        return jnp.where(r0 < n_ctx_rows, 0, 1 + (r0 - n_ctx_rows) // dec_seq)
    return row


def norm_mod(x, g, mod3, sc_chunk, sh_chunk, row_fn, out_dtype, router=None, tm=256):
    M, D = x.shape
    in_specs = [
        pl.BlockSpec((tm, D), lambda i: (i, 0)),
        pl.BlockSpec((1, D), lambda i: (0, 0)),
        pl.BlockSpec((1, 1, D), lambda i: (row_fn(i), 0, sc_chunk)),
        pl.BlockSpec((1, 1, D), lambda i: (row_fn(i), 0, sh_chunk)),
    ]
    args = [x, g.reshape(1, D), mod3, mod3]
    u_spec = pl.BlockSpec((tm, D), lambda i: (i, 0))
    u_shape = jax.ShapeDtypeStruct((M, D), out_dtype)
    if router is None:
        return pl.pallas_call(
            _norm_mod_kernel, grid=(M // tm,), in_specs=in_specs, out_specs=u_spec,
            out_shape=u_shape, compiler_params=_cparams(1), name="norm_mod")(*args)
    wr, br = router
    NE = wr.shape[1]
    in_specs += [pl.BlockSpec((D, NE), lambda i: (0, 0)), pl.BlockSpec((1, NE), lambda i: (0, 0))]
    return pl.pallas_call(
        _norm_mod_router_kernel, grid=(M // tm,), in_specs=in_specs,
        out_specs=[u_spec, pl.BlockSpec((tm, NE), lambda i: (i, 0))],
        out_shape=[u_shape, jax.ShapeDtypeStruct((M, NE), F32)],
        compiler_params=_cparams(1), name="norm_mod_router")(*args, wr, br)


def norm_plain(x, g, tm=256):
    M, D = x.shape
    return pl.pallas_call(
        _norm_plain_kernel, grid=(M // tm,),
        in_specs=[pl.BlockSpec((tm, D), lambda i: (i, 0)), pl.BlockSpec((1, D), lambda i: (0, 0))],
        out_specs=pl.BlockSpec((tm, D), lambda i: (i, 0)),
        out_shape=jax.ShapeDtypeStruct((M, D), F32),
        compiler_params=_cparams(1), name="norm_final")(x, g.reshape(1, D))


def _cast_weight(w_ref, wbf_ref, kchunk=512):
    K = w_ref.shape[0]
    for k0 in range(0, K, kchunk):
        k1 = min(K, k0 + kchunk)
        wbf_ref[k0:k1, :] = w_ref[k0:k1, :].astype(BF16)


def _mm_kernel(lhs_ref, w_ref, o_ref, wbf_ref):
    @pl.when(pl.program_id(1) == 0)
    def _():
        _cast_weight(w_ref, wbf_ref)
    o_ref[...] = jnp.dot(lhs_ref[...], wbf_ref[...], preferred_element_type=F32).astype(o_ref.dtype)


def _mm_res_kernel(lhs_ref, w_ref, res_ref, g_ref, o_ref, wbf_ref):
    @pl.when(pl.program_id(1) == 0)
    def _():
        _cast_weight(w_ref, wbf_ref)
    acc = jnp.dot(lhs_ref[...], wbf_ref[...], preferred_element_type=F32)
    o_ref[...] = res_ref[...] + g_ref[0] * acc


def mm_ws(lhs, w_all, layer, tm=512, tn=512, res=None, gate=None, out_dtype=F32):
    M, K = lhs.shape
    N = w_all.shape[-1]
    in_specs = [
        pl.BlockSpec((tm, K), lambda j, i: (i, 0)),
        pl.BlockSpec((None, K, tn), lambda j, i: (layer, 0, j)),
    ]
    out_spec = pl.BlockSpec((tm, tn), lambda j, i: (i, j))
    scratch = [pltpu.VMEM((K, tn), BF16)]
    if res is None:
        return pl.pallas_call(
            _mm_kernel, grid=(N // tn, M // tm), in_specs=in_specs, out_specs=out_spec,
            out_shape=jax.ShapeDtypeStruct((M, N), out_dtype), scratch_shapes=scratch,
            compiler_params=_cparams(2), name="mm_ws")(lhs, w_all)
    mod3, g_chunk, row_fn = gate
    ncb = mod3.shape[-1] // N
    in_specs += [
        pl.BlockSpec((tm, tn), lambda j, i: (i, j)),
        pl.BlockSpec((1, 1, tn), lambda j, i: (row_fn(i), 0, g_chunk * (N // tn) + j)),
    ]
    del ncb
    return pl.pallas_call(
        _mm_res_kernel, grid=(N // tn, M // tm), in_specs=in_specs, out_specs=out_spec,
        out_shape=jax.ShapeDtypeStruct((M, N), F32), scratch_shapes=scratch,
        input_output_aliases={2: 0},
        compiler_params=_cparams(2), name="mm_ws_res")(lhs, w_all, res, mod3)


POOL_PAD = 16


def _pool_kernel(x_ref, w_ref, s_ref, *rest, T, CH):
    o_ref, xpad = rest[-2], rest[-1]
    g = pl.program_id(1)
    C = x_ref.shape[1]
    xpad[0:POOL_PAD, :] = jnp.zeros((POOL_PAD, C), F32)
    xpad[T + POOL_PAD:T + 2 * POOL_PAD, :] = jnp.zeros((POOL_PAD, C), F32)
    for c in range(T // CH):
        xpad[POOL_PAD + c * CH:POOL_PAD + (c + 1) * CH, :] = x_ref[c * CH:(c + 1) * CH, :]
    wbf = w_ref[...].astype(BF16)
    scale = s_ref[...]
    for gi, win in enumerate(POOL_WINDOWS):
        half = win // 2

        @pl.when(g == gi)
        def _():
            for c in range(T // CH):
                base = c * CH
                acc = xpad[base + POOL_PAD - half:base + POOL_PAD - half + CH, :]
                for d in range(-half + 1, half):
                    acc = acc + xpad[base + POOL_PAD + d:base + POOL_PAD + d + CH, :]
                t = base + lax.broadcasted_iota(jnp.int32, (CH, 1), 0)
                cnt = (jnp.minimum(t + half, T) - jnp.maximum(t - half, 0)).astype(F32)
                xc = xpad[base + POOL_PAD:base + POOL_PAD + CH, :]
                pooled = (acc / cnt - xc).astype(BF16)
                y = jnp.dot(pooled, wbf, preferred_element_type=F32) * scale
                o_ref[base:base + CH, :] = y.astype(o_ref.dtype)


def pool_mixer(proj, w_pool_all, pool_scale_all, layer, mix, T, n_seq, row_blk0, mix_cols):
    C = w_pool_all.shape[-1]
    W = N_POOL_GROUPS * C
    in_specs = [
        pl.BlockSpec((T, C), lambda b, g: (row_blk0 + b, g)),
        pl.BlockSpec((None, None, C, C), lambda b, g: (layer, g, 0, 0)),
        pl.BlockSpec((None, 1, C), lambda b, g: (layer, 0, g)),
    ]
    args = [proj, w_pool_all, pool_scale_all.reshape(-1, 1, W)]
    aliases = {}
    if mix is not None:
        in_specs.append(pl.BlockSpec(memory_space=pl.ANY))
        args.append(mix)
        aliases = {3: 0}
    return pl.pallas_call(
        functools.partial(_pool_kernel, T=T, CH=min(T, 256)),
        grid=(n_seq, N_POOL_GROUPS), in_specs=in_specs,
        out_specs=pl.BlockSpec((T, C), lambda b, g: (row_blk0 + b, g)),
        out_shape=jax.ShapeDtypeStruct((proj.shape[0], mix_cols), BF16),
        scratch_shapes=[pltpu.VMEM((T + 2 * POOL_PAD, C), F32)],
        input_output_aliases=aliases,
        compiler_params=_cparams(2), name=f"pool_T{T}")(*args)


def _ctx_attn_kernel(q_ref, k_ref, v_ref, mix_ref, o_ref, *, n_heads):
    del mix_ref
    Dh = NA_HEAD_DIM
    scale = Dh ** -0.5
    for h in range(n_heads):
        sl = slice(h * Dh, (h + 1) * Dh)
        q = q_ref[:, sl].astype(BF16)
        k = k_ref[:, sl].astype(BF16)
        v = v_ref[:, sl].astype(BF16)
        s = lax.dot_general(q, k, (((1,), (1,)), ((), ())), preferred_element_type=F32) * scale
        m = jnp.max(s, axis=-1, keepdims=True)
        p = jnp.exp(s - m)
        l = jnp.sum(p, axis=-1, keepdims=True)
        o = jnp.dot(p.astype(BF16), v, preferred_element_type=F32)
        o_ref[:, sl] = (o * (1.0 / l)).astype(o_ref.dtype)


def ctx_attention(proj, mix, S, n_seq, na_width, q_col0):
    cb = 1024
    nhalf = na_width // cb
    qb, kb, vb = q_col0 // cb, (q_col0 + na_width) // cb, (q_col0 + 2 * na_width) // cb
    return pl.pallas_call(
        functools.partial(_ctx_attn_kernel, n_heads=cb // NA_HEAD_DIM),
        grid=(n_seq, nhalf),
        in_specs=[
            pl.BlockSpec((S, cb), lambda b, c: (b, qb + c)),
            pl.BlockSpec((S, cb), lambda b, c: (b, kb + c)),
            pl.BlockSpec((S, cb), lambda b, c: (b, vb + c)),
            pl.BlockSpec(memory_space=pl.ANY),
        ],
        out_specs=pl.BlockSpec((S, cb), lambda b, c: (b, qb + c)),
        out_shape=jax.ShapeDtypeStruct(mix.shape, mix.dtype),
        input_output_aliases={3: 0},
        compiler_params=_cparams(2), name="ctx_attn")(proj, proj, proj, mix)


def _natten_cases(rows):
    R, NK = NAT_R, NAT_NK
    cases = []
    for r0 in (0, R, rows - R):
        start = min(max(r0 - WIN_R // 2, 0), rows - NK)
        tiles = []
        for a in range(R):
            r = r0 + a
            rs = min(max(r - WIN_R // 2, 0), rows - WIN_R)
            row = []
            for i in range(NK):
                kr = start + i
                row.append(kr - r + (WIN_R - 1) if rs <= kr < rs + WIN_R else None)
            tiles.append(row)
        cases.append((start - r0, tiles))
    return cases


def _natten_kernel(rpb_ref, q_ref, k_ref, v_ref, ck_ref, cv_ref, mix_ref, o_ref,
                   qbf, kbf, vbf, ckbf, cvbf, tb, bm, *, rows):
    del mix_ref
    W = GRID_W
    R, NK = NAT_R, NAT_NK
    RQ, NKK = R * W, NK * W
    Dh = NA_HEAD_DIM
    scale = Dh ** -0.5
    h = pl.program_id(1)
    n_dr, n_dc = 2 * WIN_R - 1, 2 * WIN_C - 1
    T = rows * W

    for c0 in range(0, T, 512):
        qbf[c0:c0 + 512, :] = q_ref[c0:c0 + 512, :].astype(BF16)
        kbf[c0:c0 + 512, :] = k_ref[c0:c0 + 512, :].astype(BF16)
        vbf[c0:c0 + 512, :] = v_ref[c0:c0 + 512, :].astype(BF16)
    ckbf[...] = ck_ref[...].astype(BF16)
    cvbf[...] = cv_ref[...].astype(BF16)

    ci = lax.broadcasted_iota(jnp.int32, (W, W), 0)
    kci = lax.broadcasted_iota(jnp.int32, (W, W), 1)
    dci = kci - ci + (WIN_C - 1)
    cs = jnp.clip(ci - WIN_C // 2, 0, W - WIN_C)
    in_win = (kci >= cs) & (kci < cs + WIN_C)
    for dr in range(n_dr):
        t = jnp.full((W, W), NEG_BIG, F32)
        for dc in range(n_dc):
            t = jnp.where(dci == dc, rpb_ref[h * (n_dr * n_dc) + dr * n_dc + dc], t)
        tb[dr] = jnp.where(in_win, t, NEG_BIG)

    cases = _natten_cases(rows)
    neg_tile = jnp.full((W, W), NEG_BIG, F32)
    for ci_, (_, tiles) in enumerate(cases):
        for a in range(R):
            for i in range(0, NK, 2):
                pair = [tb[d] if d is not None else neg_tile for d in tiles[a][i:i + 2]]
                bm[ci_, a * W:(a + 1) * W, i * W:(i + 2) * W] = jnp.concatenate(pair, axis=1)

    nt = (((1,), (1,)), ((), ()))

    def block(q0, k0, case):
        qb = qbf[pl.ds(q0, RQ), :]
        kk = kbf[pl.ds(k0, NKK), :]
        vv = vbf[pl.ds(k0, NKK), :]
        s1 = lax.dot_general(qb, kk, nt, preferred_element_type=F32) * scale + bm[case]
        s2 = lax.dot_general(qb, ckbf[...], nt, preferred_element_type=F32) * scale
        m = jnp.maximum(jnp.max(s1, axis=-1, keepdims=True), jnp.max(s2, axis=-1, keepdims=True))
        p1 = jnp.exp(s1 - m)
        p2 = jnp.exp(s2 - m)
        l = jnp.sum(p1, axis=-1, keepdims=True) + jnp.sum(p2, axis=-1, keepdims=True)
        o = (jnp.dot(p1.astype(BF16), vv, preferred_element_type=F32)
             + jnp.dot(p2.astype(BF16), cvbf[...], preferred_element_type=F32))
        o_ref[pl.ds(q0, RQ), :] = (o * (1.0 / l)).astype(o_ref.dtype)

    block(0, (cases[0][0]) * W, 0)

    def interior(blk, carry):
        q0 = pl.multiple_of(blk * RQ, RQ)
        k0 = pl.multiple_of(q0 + cases[1][0] * W, W)
        block(q0, k0, 1)
        return carry

    lax.fori_loop(1, rows // R - 1, interior, 0)
    q_last = (rows - R) * W
    block(q_last, q_last + cases[2][0] * W, 2)


def natten(proj, cache_k4, cache_v4, rpb_all, layer, mix, T, n_seq, row_blk0, q_col0, na_width):
    Dh = NA_HEAD_DIM
    H = na_width // Dh
    P = cache_k4.shape[2]
    rows = T // GRID_W
    qb, kb, vb = q_col0 // Dh, (q_col0 + na_width) // Dh, (q_col0 + 2 * na_width) // Dh
    return pl.pallas_call(
        functools.partial(_natten_kernel, rows=rows),
        grid=(n_seq, H),
        in_specs=[
            pl.BlockSpec(memory_space=pltpu.SMEM),
            pl.BlockSpec((T, Dh), lambda b, h: (row_blk0 + b, qb + h)),
            pl.BlockSpec((T, Dh), lambda b, h: (row_blk0 + b, kb + h)),
            pl.BlockSpec((T, Dh), lambda b, h: (row_blk0 + b, vb + h)),
            pl.BlockSpec((None, None, P, Dh), lambda b, h: (b, layer, 0, h)),
            pl.BlockSpec((None, None, P, Dh), lambda b, h: (b, layer, 0, h)),
            pl.BlockSpec(memory_space=pl.ANY),
        ],
        out_specs=pl.BlockSpec((T, Dh), lambda b, h: (row_blk0 + b, qb + h)),
        out_shape=jax.ShapeDtypeStruct(mix.shape, mix.dtype),
        scratch_shapes=[
            pltpu.VMEM((T, Dh), BF16), pltpu.VMEM((T, Dh), BF16), pltpu.VMEM((T, Dh), BF16),
            pltpu.VMEM((P, Dh), BF16), pltpu.VMEM((P, Dh), BF16),
            pltpu.VMEM((2 * WIN_R - 1, GRID_W, GRID_W), F32),
            pltpu.VMEM((3, NAT_R * GRID_W, NAT_NK * GRID_W), F32),
        ],
        input_output_aliases={6: 0},
        compiler_params=_cparams(2), name="natten")(
            rpb_all[layer], proj, proj, proj, cache_k4, cache_v4, mix)


RG_PAD = 8
SCAN_ROWS = 8


def _scan8(a, b, reverse):
    row = lax.broadcasted_iota(jnp.int32, a.shape, 0)
    for s in (1, 2, 4):
        if reverse:
            sh, keep = SCAN_ROWS - s, row < SCAN_ROWS - s
        else:
            sh, keep = s, row >= s
        a_s = pltpu.roll(a, sh, 0)
        b_s = pltpu.roll(b, sh, 0)
        b = jnp.where(keep, a * b_s + b, b)
        a = jnp.where(keep, a * a_s, a)
    return a, b


def _gelu_tanh(x):
    return 0.5 * x * (1.0 + jnp.tanh(0.7978845608028654 * (x + 0.044715 * (x * x * x))))


def _rglru_kernel(x_ref, gt_ref, cw_ref, cb_ref, wa_ref, wi_ref, ba_ref, bi_ref, lam_ref, h0_ref,
                  *rest, T, CH, U):
    y_ref, hs_ref, xpad, a_f, b_f, a_b, b_b, hf_s = rest[-8:]
    C = x_ref.shape[1]
    xpad[0:RG_PAD, :] = jnp.zeros((RG_PAD, C), F32)
    xpad[T + RG_PAD:T + 2 * RG_PAD, :] = jnp.zeros((RG_PAD, C), F32)
    for c in range(T // CH):
        xpad[RG_PAD + c * CH:RG_PAD + (c + 1) * CH, :] = x_ref[c * CH:(c + 1) * CH, :]

    wcat = jnp.concatenate([wa_ref[0], wa_ref[1], wi_ref[0], wi_ref[1]], axis=1).astype(BF16)
    bcat = jnp.concatenate([ba_ref[0], ba_ref[1], bi_ref[0], bi_ref[1]], axis=1)
    z = -lam_ref[...]
    sp = jnp.maximum(z, 0.0) + jnp.log1p(jnp.exp(-jnp.abs(z)))
    cw = cw_ref[...]
    cb = cb_ref[...]
    lead = RG_PAD - CONV_W // 2

    def gates(c, carry):
        c0 = pl.multiple_of(c * CH, CH)
        win = xpad[pl.ds(c0, CH + 2 * RG_PAD), :]
        xc = cb
        for k in range(CONV_W):
            xc = xc + win[lead + k:lead + k + CH, :] * cw[k:k + 1, :]
        gz = jnp.dot(xc.astype(BF16), wcat, preferred_element_type=F32) + bcat
        for d, (a_s, b_s) in enumerate(((a_f, b_f), (a_b, b_b))):
            r = jax.nn.sigmoid(gz[:, d * C:(d + 1) * C])
            ig = jax.nn.sigmoid(gz[:, (2 + d) * C:(3 + d) * C])
            log_a = -RG_C * r * sp[d]
            a = jnp.exp(log_a)
            a_s[pl.ds(c0, CH), :] = a
            b_s[pl.ds(c0, CH), :] = jnp.sqrt(1.0 - a * a) * ig * xc
        return carry

    lax.fori_loop(0, T // CH, gates, 0)

    n_it = T // (SCAN_ROWS * U)

    def fwd(j, h):
        r0 = pl.multiple_of(j * (SCAN_ROWS * U), SCAN_ROWS * U)
        av = a_f[pl.ds(r0, SCAN_ROWS * U), :]
        bv = b_f[pl.ds(r0, SCAN_ROWS * U), :]
        outs = []
        for u in range(U):
            a8, b8 = _scan8(av[u * 8:(u + 1) * 8], bv[u * 8:(u + 1) * 8], False)
            hrows = b8 + a8 * h
            outs.append(hrows)
            h = jnp.broadcast_to(hrows[SCAN_ROWS - 1:SCAN_ROWS, :], hrows.shape)
        hf_s[pl.ds(r0, SCAN_ROWS * U), :] = jnp.concatenate(outs, axis=0)
        return h

    h0 = h0_ref[...]
    hf_last = lax.fori_loop(0, n_it, fwd, jnp.broadcast_to(h0[0:1, :], (SCAN_ROWS, C)))

    def bwd(jj, h):
        j = n_it - 1 - jj
        r0 = pl.multiple_of(j * (SCAN_ROWS * U), SCAN_ROWS * U)
        av = a_b[pl.ds(r0, SCAN_ROWS * U), :]
        bv = b_b[pl.ds(r0, SCAN_ROWS * U), :]
        outs = [None] * U
        for u in reversed(range(U)):
            a8, b8 = _scan8(av[u * 8:(u + 1) * 8], bv[u * 8:(u + 1) * 8], True)
            hrows = b8 + a8 * h
            outs[u] = hrows
            h = jnp.broadcast_to(hrows[0:1, :], hrows.shape)
        hb = jnp.concatenate(outs, axis=0)
        hf = hf_s[pl.ds(r0, SCAN_ROWS * U), :]
        gate = gt_ref[pl.ds(r0, SCAN_ROWS * U), :]
        y_ref[pl.ds(r0, SCAN_ROWS * U), :] = ((hf + hb) * _gelu_tanh(gate)).astype(y_ref.dtype)
        return h

    hb_first = lax.fori_loop(0, n_it, bwd, jnp.broadcast_to(h0[1:2, :], (SCAN_ROWS, C)))
    hs_ref[0:1, :] = hf_last[SCAN_ROWS - 1:SCAN_ROWS, :]
    hs_ref[1:2, :] = hb_first[0:1, :]


def rglru(proj, p, layer, h0, mix, T, n_seq, row_blk0, x_col0, mix_col0):
    C = RG_BLOCK_DIM
    W = h0.shape[-1]
    nb = W // C
    xb, gb, mb = x_col0 // C, (x_col0 + W) // C, mix_col0 // C
    L = p['conv_w'].shape[0]
    vec = lambda arr: arr.reshape(L, -1, 1, W)
    in_specs = [
        pl.BlockSpec((T, C), lambda b, n: (row_blk0 + b, xb + n)),
        pl.BlockSpec((T, C), lambda b, n: (row_blk0 + b, gb + n)),
        pl.BlockSpec((None, CONV_W, C), lambda b, n: (layer, 0, n)),
        pl.BlockSpec((None, 1, C), lambda b, n: (layer, 0, n)),
        pl.BlockSpec((None, 2, None, C, C), lambda b, n: (layer, 0, n, 0, 0)),
        pl.BlockSpec((None, 2, None, C, C), lambda b, n: (layer, 0, n, 0, 0)),
        pl.BlockSpec((None, 2, 1, C), lambda b, n: (layer, 0, 0, n)),
        pl.BlockSpec((None, 2, 1, C), lambda b, n: (layer, 0, 0, n)),
        pl.BlockSpec((None, 2, 1, C), lambda b, n: (layer, 0, 0, n)),
        pl.BlockSpec((None, 2, C), lambda b, n: (b, 0, n)),
        pl.BlockSpec(memory_space=pl.ANY),
    ]
    args = [proj, proj, p['conv_w'], p['conv_b'].reshape(L, 1, W), p['rg_w_a'], p['rg_w_i'],
            vec(p['rg_b_a']), vec(p['rg_b_i']), vec(p['rg_lam']), h0, mix]
    return pl.pallas_call(
        functools.partial(_rglru_kernel, T=T, CH=min(T, 256), U=4),
        grid=(n_seq, nb), in_specs=in_specs,
        out_specs=[
            pl.BlockSpec((T, C), lambda b, n: (row_blk0 + b, mb + n)),
            pl.BlockSpec((None, 2, C), lambda b, n: (b, 0, n)),
        ],
        out_shape=[jax.ShapeDtypeStruct(mix.shape, mix.dtype),
                   jax.ShapeDtypeStruct((n_seq, 2, W), F32)],
        scratch_shapes=[pltpu.VMEM((T + 2 * RG_PAD, C), F32)] + [pltpu.VMEM((T, C), F32)] * 5,
        input_output_aliases={10: 0},
        compiler_params=_cparams(2), name=f"rglru_T{T}")(*args)


def route(logits, n_blocks):
    N = logits.shape[0]
    top_v, top_i = lax.top_k(logits.reshape(N, N_EXPERT_GROUPS, GROUP_SIZE), TOP_K)
    g_sel = jnp.argmax(top_v.sum(-1), axis=-1)
    sel_v = jnp.take_along_axis(top_v, g_sel[:, None, None], axis=1)[:, 0]
    sel_i = jnp.take_along_axis(top_i, g_sel[:, None, None], axis=1)[:, 0]
    expert = (g_sel[:, None] * GROUP_SIZE + sel_i).astype(jnp.int32)
    gate = jax.nn.softmax(sel_v, axis=-1)
    A = N * TOP_K
    flat_e = expert.reshape(A)
    onehot = (flat_e[:, None] == jnp.arange(N_EXPERTS, dtype=jnp.int32)[None, :]).astype(jnp.int32)
    incl = jnp.cumsum(onehot, axis=0)
    counts = incl[-1]
    rank = jnp.sum((incl - onehot) * onehot, axis=1)
    padded = ((counts + MOE_BLOCK - 1) // MOE_BLOCK) * MOE_BLOCK
    pad_end = jnp.cumsum(padded)
    pad_start = pad_end - padded
    dest = (pad_start[flat_e] + rank).astype(jnp.int32)
    P = n_blocks * MOE_BLOCK
    flat_tok = jnp.arange(A, dtype=jnp.int32) // TOP_K
    slot_tok = jnp.zeros((P,), jnp.int32).at[dest].set(flat_tok)
    slot_gate = jnp.zeros((P,), F32).at[dest].set(gate.reshape(A))
    block_e = jnp.minimum(
        jnp.searchsorted(pad_end, jnp.arange(n_blocks, dtype=jnp.int32) * MOE_BLOCK, side='right'),
        N_EXPERTS - 1).astype(jnp.int32)
    n_active = (pad_end[-1] // MOE_BLOCK).astype(jnp.int32).reshape(1)
    return slot_tok, slot_gate, block_e, n_active, dest.reshape(N, TOP_K)


def _row_copy(src, dst, s, d, sem):
    return pltpu.make_async_copy(src.at[pl.ds(s, 1)], dst.at[pl.ds(d, 1)], sem)


def _gather_kernel(nact_ref, idx_ref, src_ref, dst_ref, sem):
    i = pl.program_id(0)
    base = i * MOE_BLOCK

    @pl.when(i < nact_ref[0])
    def _():
        def issue(r, carry):
            _row_copy(src_ref, dst_ref, idx_ref[0, 0, r], base + r, sem).start()
            return carry
        lax.fori_loop(0, MOE_BLOCK, issue, 0)
        pltpu.make_async_copy(src_ref.at[pl.ds(0, MOE_BLOCK)],
                              dst_ref.at[pl.ds(base, MOE_BLOCK)], sem).wait()


def gather_rows(src, slot_tok, n_active, n_blocks):
    D = src.shape[1]
    grid_spec = pltpu.PrefetchScalarGridSpec(
        num_scalar_prefetch=1, grid=(n_blocks,),
        in_specs=[
            pl.BlockSpec((1, 1, MOE_BLOCK), lambda i, na: (i, 0, 0), memory_space=pltpu.SMEM),
            pl.BlockSpec(memory_space=pl.ANY),
        ],
        out_specs=pl.BlockSpec(memory_space=pl.ANY),
        scratch_shapes=[pltpu.SemaphoreType.DMA(())],
    )
    return pl.pallas_call(
        _gather_kernel, grid_spec=grid_spec,
        out_shape=jax.ShapeDtypeStruct((n_blocks * MOE_BLOCK, D), src.dtype),
        compiler_params=_cparams(1), name="moe_gather")(
            n_active, slot_tok.reshape(n_blocks, 1, MOE_BLOCK), src)


def _new_expert(be_ref, i):
    return jnp.logical_or(i == 0, be_ref[i] != be_ref[jnp.maximum(i - 1, 0)])


def _moe_up_kernel(be_ref, nact_ref, xs_ref, wg_ref, wu_ref, h_ref, wg_bf, wu_bf):
    i = pl.program_id(1)

    @pl.when(_new_expert(be_ref, i))
    def _():
        _cast_weight(wg_ref, wg_bf)
        _cast_weight(wu_ref, wu_bf)

    @pl.when(i < nact_ref[0])
    def _():
        x = xs_ref[...].astype(BF16)
        hg = jnp.dot(x, wg_bf[...], preferred_element_type=F32)
        hu = jnp.dot(x, wu_bf[...], preferred_element_type=F32)
        h_ref[...] = (_silu(hg) * hu).astype(h_ref.dtype)

    @pl.when(i >= nact_ref[0])
    def _():
        h_ref[...] = jnp.zeros(h_ref.shape, h_ref.dtype)


def moe_up(xs, w_up_all, layer, block_e, n_active, fc=512):
    P, D = xs.shape
    F = w_up_all.shape[-1] // 2
    n_blocks = P // MOE_BLOCK
    nfc = F // fc
    grid_spec = pltpu.PrefetchScalarGridSpec(
        num_scalar_prefetch=2, grid=(nfc, n_blocks),
        in_specs=[
            pl.BlockSpec((MOE_BLOCK, D), lambda j, i, be, na: (i, 0)),
            pl.BlockSpec((None, None, D, fc), lambda j, i, be, na: (layer, be[i], 0, j)),
            pl.BlockSpec((None, None, D, fc), lambda j, i, be, na: (layer, be[i], 0, nfc + j)),
        ],
        out_specs=pl.BlockSpec((MOE_BLOCK, fc), lambda j, i, be, na: (i, j)),
        scratch_shapes=[pltpu.VMEM((D, fc), BF16), pltpu.VMEM((D, fc), BF16)],
    )
    return pl.pallas_call(
        _moe_up_kernel, grid_spec=grid_spec,
        out_shape=jax.ShapeDtypeStruct((P, F), BF16),
        compiler_params=_cparams(2), name="moe_up")(block_e, n_active, xs, w_up_all, w_up_all)


def _moe_down_kernel(be_ref, nact_ref, h_ref, wd_ref, gate_ref, y_ref, wd_bf):
    i = pl.program_id(1)

    @pl.when(_new_expert(be_ref, i))
    def _():
        _cast_weight(wd_ref, wd_bf)

    @pl.when(i < nact_ref[0])
    def _():
        y = jnp.dot(h_ref[...], wd_bf[...], preferred_element_type=F32)
        y_ref[...] = y * gate_ref[...]

    @pl.when(i >= nact_ref[0])
    def _():
        y_ref[...] = jnp.zeros(y_ref.shape, y_ref.dtype)


def moe_down(h, w_down_all, layer, slot_gate, block_e, n_active, tn=1024):
    P, F = h.shape
    D = w_down_all.shape[-1]
    n_blocks = P // MOE_BLOCK
    grid_spec = pltpu.PrefetchScalarGridSpec(
        num_scalar_prefetch=2, grid=(D // tn, n_blocks),
        in_specs=[
            pl.BlockSpec((MOE_BLOCK, F), lambda j, i, be, na: (i, 0)),
            pl.BlockSpec((None, None, F, tn), lambda j, i, be, na: (layer, be[i], 0, j)),
            pl.BlockSpec((MOE_BLOCK, 1), lambda j, i, be, na: (i, 0)),
        ],
        out_specs=pl.BlockSpec((MOE_BLOCK, tn), lambda j, i, be, na: (i, j)),
        scratch_shapes=[pltpu.VMEM((F, tn), BF16)],
    )
    return pl.pallas_call(
        _moe_down_kernel, grid_spec=grid_spec,
        out_shape=jax.ShapeDtypeStruct((P, D), F32),
        compiler_params=_cparams(2), name="moe_down")(
            block_e, n_active, h, w_down_all, slot_gate.reshape(P, 1))


def _combine_kernel(pos_ref, x_ref, g_ref, y_ref, o_ref, buf, sem, *, tm):
    def issue(r, carry):
        for k in range(TOP_K):
            _row_copy(y_ref, buf.at[k], pos_ref[0, 0, TOP_K * r + k], r, sem).start()
        return carry
    lax.fori_loop(0, tm, issue, 0)
    for k in range(TOP_K):
        pltpu.make_async_copy(y_ref.at[pl.ds(0, tm)], buf.at[k], sem).wait()
    o_ref[...] = x_ref[...] + g_ref[0] * (buf[0] + buf[1])


def moe_combine(x, y, pos, mod3, g_chunk, row_fn, tm=128):
    M, D = x.shape
    return pl.pallas_call(
        functools.partial(_combine_kernel, tm=tm),
        grid=(M // tm,),
        in_specs=[
            pl.BlockSpec((1, 1, TOP_K * tm), lambda i: (i, 0, 0), memory_space=pltpu.SMEM),
            pl.BlockSpec((tm, D), lambda i: (i, 0)),
            pl.BlockSpec((1, 1, D), lambda i: (row_fn(i), 0, g_chunk)),
            pl.BlockSpec(memory_space=pl.ANY),
        ],
        out_specs=pl.BlockSpec((tm, D), lambda i: (i, 0)),
        out_shape=jax.ShapeDtypeStruct((M, D), F32),
        scratch_shapes=[pltpu.VMEM((TOP_K, tm, D), F32), pltpu.SemaphoreType.DMA(())],
        input_output_aliases={1: 0},
        compiler_params=_cparams(1), name="moe_combine")(
            pos.reshape(M // tm, 1, TOP_K * tm), x, mod3, y)


def kernel(x_prompt, x_sample, cache_k, cache_v, state_rglru, c, c_ctx, w_ada, b_ada, norm_attn,
           norm_ffn, norm_final, w_in, w_pool, pool_scale, rpb, conv_w, conv_b, rg_w_a, rg_b_a,
           rg_w_i, rg_b_i, rg_lam, w_out, w_router, b_router, w_up, w_down):
    B, S, D = x_prompt.shape
    DB, T, _ = x_sample.shape
    L = w_in.shape[0]
    pool_w = w_pool.shape[1] * w_pool.shape[2]
    na_w = cache_k.shape[3] * cache_k.shape[4]
    rg_w = rg_lam.shape[-1]
    n_ctx = B * S
    n_tok = n_ctx + DB * T
    q_col0 = pool_w
    x_col0 = pool_w + 3 * na_w

    n_cond = 8
    cond = jnp.concatenate([c_ctx[None, :], c, jnp.zeros((n_cond - 1 - DB, D), F32)], axis=0)
    mod = ada_mod(cond, w_ada, b_ada)

    x = jnp.concatenate([x_prompt.reshape(n_ctx, D), x_sample.reshape(DB * T, D)], axis=0)

    wr = jnp.zeros((D, 128), F32).at[:, :N_EXPERTS].set(w_router).astype(BF16)
    br = jnp.zeros((1, 128), F32).at[0, :N_EXPERTS].set(b_router)
    n_blocks = (n_tok * TOP_K) // MOE_BLOCK + N_EXPERTS

    ck4 = cache_k.reshape(DB, L, cache_k.shape[2], na_w)
    cv4 = cache_v.reshape(DB, L, cache_v.shape[2], na_w)
    rpb_flat = rpb.reshape(L, -1)
    rg_params = dict(conv_w=conv_w, conv_b=conv_b, rg_w_a=rg_w_a, rg_w_i=rg_w_i, rg_b_a=rg_b_a,
                     rg_b_i=rg_b_i, rg_lam=rg_lam)
    zeros_h = jnp.zeros((B, 2, rg_w), F32)

    new_k, new_v, new_h = [], [], []
    for l in range(L):
        mod3 = mod[l].reshape(n_cond, 1, 6 * D)
        row256 = _cond_row_fn(256, n_ctx, T)
        row512 = _cond_row_fn(512, n_ctx, T)
        row128 = _cond_row_fn(128, n_ctx, T)

        u = norm_mod(x, norm_attn[l], mod3, 1, 0, row256, BF16)
        proj = mm_ws(u, w_in, l)

        kv = proj[:n_ctx, q_col0 + na_w:q_col0 + 3 * na_w]
        new_k.append(kv[:, :na_w].reshape(B, S, -1, NA_HEAD_DIM))
        new_v.append(kv[:, na_w:].reshape(B, S, -1, NA_HEAD_DIM))

        mix = pool_mixer(proj, w_pool, pool_scale, l, None, S, B, 0, D)
        mix = pool_mixer(proj, w_pool, pool_scale, l, mix, T, DB, n_ctx // T, D)
        mix = ctx_attention(proj, mix, S, B, na_w, q_col0)
        mix = natten(proj, ck4, cv4, rpb_flat, l, mix, T, DB, n_ctx // T, q_col0, na_w)
        mix, h_ctx = rglru(proj, rg_params, l, zeros_h, mix, S, B, 0, x_col0, pool_w + na_w)
        mix, _ = rglru(proj, rg_params, l, state_rglru[:, l], mix, T, DB, n_ctx // T, x_col0,
                       pool_w + na_w)
        new_h.append(h_ctx)

        x = mm_ws(mix, w_out, l, res=x, gate=(mod3, 2, row512))

        u2, logits = norm_mod(x, norm_ffn[l], mod3, 4, 3, row256, F32, router=(wr, br))
        slot_tok, slot_gate, block_e, n_active, pos = route(logits[:, :N_EXPERTS], n_blocks)
        xs = gather_rows(u2, slot_tok, n_active, n_blocks)
        hmid = moe_up(xs, w_up, l, block_e, n_active)
        y = moe_down(hmid, w_down, l, slot_gate, block_e, n_active)
        x = moe_combine(x, y, pos, mod3, 5, row128)

    yf = norm_plain(x, norm_final)
    y_prompt = yf[:n_ctx].reshape(B, S, D)
    y_sample = yf[n_ctx:].reshape(DB, T, D)
    return (y_prompt, y_sample, jnp.stack(new_k, axis=1), jnp.stack(new_v, axis=1),
            jnp.stack(new_h, axis=1))
```

```python
import functools

import jax
import jax.numpy as jnp
from jax import lax
from jax.experimental import pallas as pl
from jax.experimental.pallas import tpu as pltpu

F32 = jnp.float32
BF16 = jnp.bfloat16

EPS = 1e-6
GRID_W = 64
POOL_WINDOWS = (2, 4, 8, 16)
N_POOL_GROUPS = 4
WIN_R = 8
WIN_C = 16
NA_HEAD_DIM = 128
RG_BLOCK_DIM = 128
CONV_W = 4
RG_C = 8.0
N_EXPERTS = 16
N_EXPERT_GROUPS = 4
GROUP_SIZE = N_EXPERTS // N_EXPERT_GROUPS
TOP_K = 2
MOE_BLOCK = 512

NEG_BIG = -1e30
VMEM_LIMIT_BYTES = 60 * 1024 * 1024

NAT_R = 4
NAT_NK = 12


def _cparams(n_axes):
    return pltpu.CompilerParams(
        dimension_semantics=("arbitrary",) * n_axes, vmem_limit_bytes=VMEM_LIMIT_BYTES)


def _silu(x):
    return x * jax.nn.sigmoid(x)


def _sigmoid_tanh(x):
    return 0.5 * jnp.tanh(0.5 * x) + 0.5


def _ada_kernel(c_ref, w_ref, b_ref, o_ref):
    s = _silu(c_ref[...]).astype(BF16)
    w = w_ref[...].astype(BF16)
    o_ref[...] = jnp.dot(s, w, preferred_element_type=F32) + b_ref[...]


def ada_mod(cond, w_ada, b_ada, tn=512):
    L, D, N = w_ada.shape
    R = cond.shape[0]
    return pl.pallas_call(
        _ada_kernel,
        grid=(L, N // tn),
        in_specs=[
            pl.BlockSpec((R, D), lambda l, j: (0, 0)),
            pl.BlockSpec((None, D, tn), lambda l, j: (l, 0, j)),
            pl.BlockSpec((None, 1, tn), lambda l, j: (l, 0, j)),
        ],
        out_specs=pl.BlockSpec((None, R, tn), lambda l, j: (l, 0, j)),
        out_shape=jax.ShapeDtypeStruct((L, R, N), F32),
        compiler_params=_cparams(2),
        name="ada_mod",
    )(cond, w_ada, b_ada.reshape(L, 1, N))


def _rms(x, g):
    ms = jnp.mean(x * x, axis=-1, keepdims=True)
    return x * lax.rsqrt(ms + EPS) * g


def _norm_mod_kernel(x_ref, g_ref, sc_ref, sh_ref, u_ref):
    y = _rms(x_ref[...], g_ref[...])
    u_ref[...] = (y * (1.0 + sc_ref[0]) + sh_ref[0]).astype(u_ref.dtype)


def _norm_mod_router_kernel(x_ref, g_ref, sc_ref, sh_ref, wr_ref, br_ref, u_ref, lg_ref):
    y = _rms(x_ref[...], g_ref[...])
    u = y * (1.0 + sc_ref[0]) + sh_ref[0]
    u_ref[...] = u.astype(u_ref.dtype)
    lg_ref[...] = jnp.dot(u.astype(BF16), wr_ref[...], preferred_element_type=F32) + br_ref[...]


def _cond_row_fn(tm, n_ctx_rows, dec_seq):
    def row(i):
        r0 = i * tm
        return jnp.where(r0 < n_ctx_rows, 0, 1 + (r0 - n_ctx_rows) // dec_seq)
    return row


def norm_mod(x, g, mod3, sc_chunk, sh_chunk, row_fn, out_dtype, router=None, tm=256):
    M, D = x.shape
    in_specs = [
        pl.BlockSpec((tm, D), lambda i: (i, 0)),
        pl.BlockSpec((1, D), lambda i: (0, 0)),
        pl.BlockSpec((1, 1, D), lambda i: (row_fn(i), 0, sc_chunk)),
        pl.BlockSpec((1, 1, D), lambda i: (row_fn(i), 0, sh_chunk)),
    ]
    args = [x, g.reshape(1, D), mod3, mod3]
    u_spec = pl.BlockSpec((tm, D), lambda i: (i, 0))
    u_shape = jax.ShapeDtypeStruct((M, D), out_dtype)
    if router is None:
        return pl.pallas_call(
            _norm_mod_kernel, grid=(M // tm,), in_specs=in_specs, out_specs=u_spec,
            out_shape=u_shape, compiler_params=_cparams(1), name="norm_mod")(*args)
    wr, br = router
    NE = wr.shape[1]
    in_specs += [pl.BlockSpec((D, NE), lambda i: (0, 0)), pl.BlockSpec((1, NE), lambda i: (0, 0))]
    return pl.pallas_call(
        _norm_mod_router_kernel, grid=(M // tm,), in_specs=in_specs,
        out_specs=[u_spec, pl.BlockSpec((tm, NE), lambda i: (i, 0))],
        out_shape=[u_shape, jax.ShapeDtypeStruct((M, NE), F32)],
        compiler_params=_cparams(1), name="norm_mod_router")(*args, wr, br)


def _cast_weight(w_ref, wbf_ref, kchunk=512):
    K = w_ref.shape[0]
    for k0 in range(0, K, kchunk):
        k1 = min(K, k0 + kchunk)
        wbf_ref[k0:k1, :] = w_ref[k0:k1, :].astype(BF16)


def _mm_kernel(lhs_ref, w_ref, o_ref, wbf_ref):
    @pl.when(pl.program_id(1) == 0)
    def _():
        _cast_weight(w_ref, wbf_ref)
    o_ref[...] = jnp.dot(lhs_ref[...], wbf_ref[...], preferred_element_type=F32).astype(o_ref.dtype)


def _mm_res_kernel(lhs_ref, w_ref, res_ref, g_ref, o_ref, wbf_ref):
    @pl.when(pl.program_id(1) == 0)
    def _():
        _cast_weight(w_ref, wbf_ref)
    acc = jnp.dot(lhs_ref[...], wbf_ref[...], preferred_element_type=F32)
    o_ref[...] = res_ref[...] + g_ref[0] * acc


def mm_ws(lhs, w_all, layer, tm, tn, res=None, gate=None, out_dtype=F32):
    M, K = lhs.shape
    N = w_all.shape[-1]
    in_specs = [
        pl.BlockSpec((tm, K), lambda j, i: (i, 0)),
        pl.BlockSpec((None, K, tn), lambda j, i: (layer, 0, j), pipeline_mode=pl.Buffered(1)),
    ]
    out_spec = pl.BlockSpec((tm, tn), lambda j, i: (i, j))
    scratch = [pltpu.VMEM((K, tn), BF16)]
    if res is None:
        return pl.pallas_call(
            _mm_kernel, grid=(N // tn, M // tm), in_specs=in_specs, out_specs=out_spec,
            out_shape=jax.ShapeDtypeStruct((M, N), out_dtype), scratch_shapes=scratch,
            compiler_params=_cparams(2), name="mm_ws")(lhs, w_all)
    mod3, g_chunk, row_fn = gate
    in_specs += [
        pl.BlockSpec((tm, tn), lambda j, i: (i, j)),
        pl.BlockSpec((1, 1, tn), lambda j, i: (row_fn(i), 0, g_chunk * (N // tn) + j)),
    ]
    return pl.pallas_call(
        _mm_res_kernel, grid=(N // tn, M // tm), in_specs=in_specs, out_specs=out_spec,
        out_shape=jax.ShapeDtypeStruct((M, N), F32), scratch_shapes=scratch,
        input_output_aliases={2: 0},
        compiler_params=_cparams(2), name="mm_ws_res")(lhs, w_all, res, mod3)


POOL_PAD = 16


def _pool_kernel(x_ref, w_ref, s_ref, *rest, T, CH):
    o_ref, xpad = rest[-2], rest[-1]
    g = pl.program_id(1)
    C = x_ref.shape[1]
    xpad[0:POOL_PAD, :] = jnp.zeros((POOL_PAD, C), F32)
    xpad[T + POOL_PAD:T + 2 * POOL_PAD, :] = jnp.zeros((POOL_PAD, C), F32)
    for c in range(T // CH):
        xpad[POOL_PAD + c * CH:POOL_PAD + (c + 1) * CH, :] = x_ref[c * CH:(c + 1) * CH, :]
    wbf = w_ref[...].astype(BF16)
    scale = s_ref[...]
    for gi, win in enumerate(POOL_WINDOWS):
        half = win // 2

        @pl.when(g == gi)
        def _():
            for c in range(T // CH):
                base = c * CH
                acc = xpad[base + POOL_PAD - half:base + POOL_PAD - half + CH, :]
                for d in range(-half + 1, half):
                    acc = acc + xpad[base + POOL_PAD + d:base + POOL_PAD + d + CH, :]
                t = base + lax.broadcasted_iota(jnp.int32, (CH, 1), 0)
                cnt = (jnp.minimum(t + half, T) - jnp.maximum(t - half, 0)).astype(F32)
                xc = xpad[base + POOL_PAD:base + POOL_PAD + CH, :]
                pooled = (acc / cnt - xc).astype(BF16)
                y = jnp.dot(pooled, wbf, preferred_element_type=F32) * scale
                o_ref[base:base + CH, :] = y.astype(o_ref.dtype)


def pool_mixer(proj, w_pool_all, pool_scale_all, layer, mix, T, n_seq, row_blk0, mix_cols):
    C = w_pool_all.shape[-1]
    W = N_POOL_GROUPS * C
    in_specs = [
        pl.BlockSpec((T, C), lambda b, g: (row_blk0 + b, g)),
        pl.BlockSpec((None, None, C, C), lambda b, g: (layer, g, 0, 0)),
        pl.BlockSpec((None, 1, C), lambda b, g: (layer, 0, g)),
    ]
    args = [proj, w_pool_all, pool_scale_all.reshape(-1, 1, W)]
    aliases = {}
    if mix is not None:
        in_specs.append(pl.BlockSpec(memory_space=pl.ANY))
        args.append(mix)
        aliases = {3: 0}
    return pl.pallas_call(
        functools.partial(_pool_kernel, T=T, CH=min(T, 256)),
        grid=(n_seq, N_POOL_GROUPS), in_specs=in_specs,
        out_specs=pl.BlockSpec((T, C), lambda b, g: (row_blk0 + b, g)),
        out_shape=jax.ShapeDtypeStruct((proj.shape[0], mix_cols), BF16),
        scratch_shapes=[pltpu.VMEM((T + 2 * POOL_PAD, C), F32)],
        input_output_aliases=aliases,
        compiler_params=_cparams(2), name=f"pool_T{T}")(*args)


def _ctx_attn_kernel(q_ref, k_ref, v_ref, mix_ref, o_ref, *, n_heads):
    del mix_ref
    Dh = NA_HEAD_DIM
    scale = Dh ** -0.5
    for h in range(n_heads):
        sl = slice(h * Dh, (h + 1) * Dh)
        q = q_ref[:, sl].astype(BF16)
        k = k_ref[:, sl].astype(BF16)
        v = v_ref[:, sl].astype(BF16)
        s = lax.dot_general(q, k, (((1,), (1,)), ((), ())), preferred_element_type=F32) * scale
        m = jnp.max(s, axis=-1, keepdims=True)
        p = jnp.exp(s - m)
        l = jnp.sum(p, axis=-1, keepdims=True)
        o = jnp.dot(p.astype(BF16), v, preferred_element_type=F32)
        o_ref[:, sl] = (o * (1.0 / l)).astype(o_ref.dtype)


def ctx_attention(proj, mix, S, n_seq, na_width, q_col0):
    cb = 1024
    nhalf = na_width // cb
    qb, kb, vb = q_col0 // cb, (q_col0 + na_width) // cb, (q_col0 + 2 * na_width) // cb
    return pl.pallas_call(
        functools.partial(_ctx_attn_kernel, n_heads=cb // NA_HEAD_DIM),
        grid=(n_seq, nhalf),
        in_specs=[
            pl.BlockSpec((S, cb), lambda b, c: (b, qb + c)),
            pl.BlockSpec((S, cb), lambda b, c: (b, kb + c)),
            pl.BlockSpec((S, cb), lambda b, c: (b, vb + c)),
            pl.BlockSpec(memory_space=pl.ANY),
        ],
        out_specs=pl.BlockSpec((S, cb), lambda b, c: (b, qb + c)),
        out_shape=jax.ShapeDtypeStruct(mix.shape, mix.dtype),
        input_output_aliases={3: 0},
        compiler_params=_cparams(2), name="ctx_attn")(proj, proj, proj, mix)


def _natten_cases(rows):
    R, NK = NAT_R, NAT_NK
    cases = []
    for r0 in (0, R, rows - R):
        start = min(max(r0 - WIN_R // 2, 0), rows - NK)
        tiles = []
        for a in range(R):
            r = r0 + a
            rs = min(max(r - WIN_R // 2, 0), rows - WIN_R)
            row = []
            for i in range(NK):
                kr = start + i
                row.append(kr - r + (WIN_R - 1) if rs <= kr < rs + WIN_R else None)
            tiles.append(row)
        cases.append((start - r0, tiles))
    return cases


def _natten_kernel(rpb_ref, q_ref, k_ref, v_ref, ck_ref, cv_ref, mix_ref, o_ref,
                   qbf, kbf, vbf, ckbf, cvbf, tb, bm, *, rows):
    del mix_ref
    W = GRID_W
    R, NK = NAT_R, NAT_NK
    RQ, NKK = R * W, NK * W
    Dh = NA_HEAD_DIM
    scale = Dh ** -0.5
    h = pl.program_id(1)
    n_dr, n_dc = 2 * WIN_R - 1, 2 * WIN_C - 1
    T = rows * W

    for c0 in range(0, T, 512):
        qbf[c0:c0 + 512, :] = q_ref[c0:c0 + 512, :].astype(BF16)
        kbf[c0:c0 + 512, :] = k_ref[c0:c0 + 512, :].astype(BF16)
        vbf[c0:c0 + 512, :] = v_ref[c0:c0 + 512, :].astype(BF16)
    ckbf[...] = ck_ref[...].astype(BF16)
    cvbf[...] = cv_ref[...].astype(BF16)

    ci = lax.broadcasted_iota(jnp.int32, (W, W), 0)
    kci = lax.broadcasted_iota(jnp.int32, (W, W), 1)
    dci = kci - ci + (WIN_C - 1)
    cs = jnp.clip(ci - WIN_C // 2, 0, W - WIN_C)
    in_win = (kci >= cs) & (kci < cs + WIN_C)
    for dr in range(n_dr):
        t = jnp.full((W, W), NEG_BIG, F32)
        for dc in range(n_dc):
            t = jnp.where(dci == dc, rpb_ref[h * (n_dr * n_dc) + dr * n_dc + dc], t)
        tb[dr] = jnp.where(in_win, t, NEG_BIG)

    cases = _natten_cases(rows)
    neg_tile = jnp.full((W, W), NEG_BIG, F32)
    for ci_, (_, tiles) in enumerate(cases):
        for a in range(R):
            for i in range(0, NK, 2):
                pair = [tb[d] if d is not None else neg_tile for d in tiles[a][i:i + 2]]
                bm[ci_, a * W:(a + 1) * W, i * W:(i + 2) * W] = jnp.concatenate(pair, axis=1)

    nt = (((1,), (1,)), ((), ()))

    def block(q0, k0, case):
        qb = qbf[pl.ds(q0, RQ), :]
        kk = kbf[pl.ds(k0, NKK), :]
        vv = vbf[pl.ds(k0, NKK), :]
        s1 = lax.dot_general(qb, kk, nt, preferred_element_type=F32) * scale + bm[case]
        s2 = lax.dot_general(qb, ckbf[...], nt, preferred_element_type=F32) * scale
        m = jnp.maximum(jnp.max(s1, axis=-1, keepdims=True), jnp.max(s2, axis=-1, keepdims=True))
        p1 = jnp.exp(s1 - m)
        p2 = jnp.exp(s2 - m)
        l = jnp.sum(p1, axis=-1, keepdims=True) + jnp.sum(p2, axis=-1, keepdims=True)
        o = (jnp.dot(p1.astype(BF16), vv, preferred_element_type=F32)
             + jnp.dot(p2.astype(BF16), cvbf[...], preferred_element_type=F32))
        o_ref[pl.ds(q0, RQ), :] = (o * (1.0 / l)).astype(o_ref.dtype)

    block(0, (cases[0][0]) * W, 0)

    def interior(blk, carry):
        q0 = pl.multiple_of(blk * RQ, RQ)
        k0 = pl.multiple_of(q0 + cases[1][0] * W, W)
        block(q0, k0, 1)
        return carry

    lax.fori_loop(1, rows // R - 1, interior, 0)
    q_last = (rows - R) * W
    block(q_last, q_last + cases[2][0] * W, 2)


def natten(proj, cache_k4, cache_v4, rpb_all, layer, mix, T, n_seq, row_blk0, q_col0, na_width):
    Dh = NA_HEAD_DIM
    H = na_width // Dh
    P = cache_k4.shape[2]
    rows = T // GRID_W
    qb, kb, vb = q_col0 // Dh, (q_col0 + na_width) // Dh, (q_col0 + 2 * na_width) // Dh
    return pl.pallas_call(
        functools.partial(_natten_kernel, rows=rows),
        grid=(n_seq, H),
        in_specs=[
            pl.BlockSpec(memory_space=pltpu.SMEM),
            pl.BlockSpec((T, Dh), lambda b, h: (row_blk0 + b, qb + h)),
            pl.BlockSpec((T, Dh), lambda b, h: (row_blk0 + b, kb + h)),
            pl.BlockSpec((T, Dh), lambda b, h: (row_blk0 + b, vb + h)),
            pl.BlockSpec((None, None, P, Dh), lambda b, h: (b, layer, 0, h)),
            pl.BlockSpec((None, None, P, Dh), lambda b, h: (b, layer, 0, h)),
            pl.BlockSpec(memory_space=pl.ANY),
        ],
        out_specs=pl.BlockSpec((T, Dh), lambda b, h: (row_blk0 + b, qb + h)),
        out_shape=jax.ShapeDtypeStruct(mix.shape, mix.dtype),
        scratch_shapes=[
            pltpu.VMEM((T, Dh), BF16), pltpu.VMEM((T, Dh), BF16), pltpu.VMEM((T, Dh), BF16),
            pltpu.VMEM((P, Dh), BF16), pltpu.VMEM((P, Dh), BF16),
            pltpu.VMEM((2 * WIN_R - 1, GRID_W, GRID_W), F32),
            pltpu.VMEM((3, NAT_R * GRID_W, NAT_NK * GRID_W), F32),
        ],
        input_output_aliases={6: 0},
        compiler_params=_cparams(2), name="natten")(
            rpb_all[layer], proj, proj, proj, cache_k4, cache_v4, mix)


RG_PAD = 8
SCAN_ROWS = 8


def _scan8(a, b, reverse):
    row = lax.broadcasted_iota(jnp.int32, a.shape, 0)
    for s in (1, 2, 4):
        if reverse:
            sh, keep = SCAN_ROWS - s, row < SCAN_ROWS - s
        else:
            sh, keep = s, row >= s
        a_s = pltpu.roll(a, sh, 0)
        b_s = pltpu.roll(b, sh, 0)
        b = jnp.where(keep, a * b_s + b, b)
        a = jnp.where(keep, a * a_s, a)
    return a, b


def _gelu_tanh(x):
    return 0.5 * x * (1.0 + jnp.tanh(0.7978845608028654 * (x + 0.044715 * (x * x * x))))


def _rglru_kernel(x_ref, gt_ref, cw_ref, cb_ref, wa_ref, wi_ref, ba_ref, bi_ref, lam_ref, h0_ref,
                  *rest, T, CH, U):
    y_ref, hs_ref, xpad, a_f, b_f, a_b, b_b, hf_s = rest[-8:]
    C = x_ref.shape[1]
    xpad[0:RG_PAD, :] = jnp.zeros((RG_PAD, C), F32)
    xpad[T + RG_PAD:T + 2 * RG_PAD, :] = jnp.zeros((RG_PAD, C), F32)
    for c in range(T // CH):
        xpad[RG_PAD + c * CH:RG_PAD + (c + 1) * CH, :] = x_ref[c * CH:(c + 1) * CH, :]

    wcat = jnp.concatenate([wa_ref[0], wa_ref[1], wi_ref[0], wi_ref[1]], axis=1).astype(BF16)
    bcat = jnp.concatenate([ba_ref[0], ba_ref[1], bi_ref[0], bi_ref[1]], axis=1)
    z = -lam_ref[...]
    sp = jnp.maximum(z, 0.0) + jnp.log1p(jnp.exp(-jnp.abs(z)))
    cw = cw_ref[...]
    cb = cb_ref[...]
    lead = RG_PAD - CONV_W // 2

    def gates(c, carry):
        c0 = pl.multiple_of(c * CH, CH)
        win = xpad[pl.ds(c0, CH + 2 * RG_PAD), :]
        xc = cb
        for k in range(CONV_W):
            xc = xc + win[lead + k:lead + k + CH, :] * cw[k:k + 1, :]
        gz = jnp.dot(xc.astype(BF16), wcat, preferred_element_type=F32) + bcat
        for d, (a_s, b_s) in enumerate(((a_f, b_f), (a_b, b_b))):
            r = _sigmoid_tanh(gz[:, d * C:(d + 1) * C])
            ig = _sigmoid_tanh(gz[:, (2 + d) * C:(3 + d) * C])
            log_a = -RG_C * r * sp[d]
            a = jnp.exp(log_a)
            a_s[pl.ds(c0, CH), :] = a
            b_s[pl.ds(c0, CH), :] = jnp.sqrt(1.0 - a * a) * ig * xc
        return carry

    lax.fori_loop(0, T // CH, gates, 0)

    n_it = T // (SCAN_ROWS * U)

    def fwd(j, h):
        r0 = pl.multiple_of(j * (SCAN_ROWS * U), SCAN_ROWS * U)
        av = a_f[pl.ds(r0, SCAN_ROWS * U), :]
        bv = b_f[pl.ds(r0, SCAN_ROWS * U), :]
        outs = []
        for u in range(U):
            a8, b8 = _scan8(av[u * 8:(u + 1) * 8], bv[u * 8:(u + 1) * 8], False)
            hrows = b8 + a8 * h
            outs.append(hrows)
            h = jnp.broadcast_to(hrows[SCAN_ROWS - 1:SCAN_ROWS, :], hrows.shape)
        hf_s[pl.ds(r0, SCAN_ROWS * U), :] = jnp.concatenate(outs, axis=0)
        return h

    h0 = h0_ref[...]
    hf_last = lax.fori_loop(0, n_it, fwd, jnp.broadcast_to(h0[0:1, :], (SCAN_ROWS, C)))

    def bwd(jj, h):
        j = n_it - 1 - jj
        r0 = pl.multiple_of(j * (SCAN_ROWS * U), SCAN_ROWS * U)
        av = a_b[pl.ds(r0, SCAN_ROWS * U), :]
        bv = b_b[pl.ds(r0, SCAN_ROWS * U), :]
        outs = [None] * U
        for u in reversed(range(U)):
            a8, b8 = _scan8(av[u * 8:(u + 1) * 8], bv[u * 8:(u + 1) * 8], True)
            hrows = b8 + a8 * h
            outs[u] = hrows
            h = jnp.broadcast_to(hrows[0:1, :], hrows.shape)
        hb = jnp.concatenate(outs, axis=0)
        hf = hf_s[pl.ds(r0, SCAN_ROWS * U), :]
        gate = gt_ref[pl.ds(r0, SCAN_ROWS * U), :]
        y_ref[pl.ds(r0, SCAN_ROWS * U), :] = ((hf + hb) * _gelu_tanh(gate)).astype(y_ref.dtype)
        return h

    hb_first = lax.fori_loop(0, n_it, bwd, jnp.broadcast_to(h0[1:2, :], (SCAN_ROWS, C)))
    hs_ref[0:1, :] = hf_last[SCAN_ROWS - 1:SCAN_ROWS, :]
    hs_ref[1:2, :] = hb_first[0:1, :]


def rglru(proj, p, layer, h0, mix, T, n_seq, row_blk0, x_col0, mix_col0):
    C = RG_BLOCK_DIM
    W = h0.shape[-1]
    nb = W // C
    xb, gb, mb = x_col0 // C, (x_col0 + W) // C, mix_col0 // C
    L = p['conv_w'].shape[0]
    vec = lambda arr: arr.reshape(L, -1, 1, W)
    in_specs = [
        pl.BlockSpec((T, C), lambda b, n: (row_blk0 + b, xb + n)),
        pl.BlockSpec((T, C), lambda b, n: (row_blk0 + b, gb + n)),
        pl.BlockSpec((None, CONV_W, C), lambda b, n: (layer, 0, n)),
        pl.BlockSpec((None, 1, C), lambda b, n: (layer, 0, n)),
        pl.BlockSpec((None, 2, None, C, C), lambda b, n: (layer, 0, n, 0, 0)),
        pl.BlockSpec((None, 2, None, C, C), lambda b, n: (layer, 0, n, 0, 0)),
        pl.BlockSpec((None, 2, 1, C), lambda b, n: (layer, 0, 0, n)),
        pl.BlockSpec((None, 2, 1, C), lambda b, n: (layer, 0, 0, n)),
        pl.BlockSpec((None, 2, 1, C), lambda b, n: (layer, 0, 0, n)),
        pl.BlockSpec((None, 2, C), lambda b, n: (b, 0, n)),
        pl.BlockSpec(memory_space=pl.ANY),
    ]
    args = [proj, proj, p['conv_w'], p['conv_b'].reshape(L, 1, W), p['rg_w_a'], p['rg_w_i'],
            vec(p['rg_b_a']), vec(p['rg_b_i']), vec(p['rg_lam']), h0, mix]
    return pl.pallas_call(
        functools.partial(_rglru_kernel, T=T, CH=min(T, 256), U=4),
        grid=(n_seq, nb), in_specs=in_specs,
        out_specs=[
            pl.BlockSpec((T, C), lambda b, n: (row_blk0 + b, mb + n)),
            pl.BlockSpec((None, 2, C), lambda b, n: (b, 0, n)),
        ],
        out_shape=[jax.ShapeDtypeStruct(mix.shape, mix.dtype),
                   jax.ShapeDtypeStruct((n_seq, 2, W), F32)],
        scratch_shapes=[pltpu.VMEM((T + 2 * RG_PAD, C), F32)] + [pltpu.VMEM((T, C), F32)] * 5,
        input_output_aliases={10: 0},
        compiler_params=_cparams(2), name=f"rglru_T{T}")(*args)


def route(logits, n_blocks):
    N = logits.shape[0]
    lg = logits.reshape(N, N_EXPERT_GROUPS, GROUP_SIZE)
    lane = jnp.arange(GROUP_SIZE, dtype=jnp.int32)
    i1 = jnp.argmax(lg, axis=-1).astype(jnp.int32)
    v1 = jnp.max(lg, axis=-1)
    rest = jnp.where(lane == i1[..., None], -jnp.inf, lg)
    i2 = jnp.argmax(rest, axis=-1).astype(jnp.int32)
    v2 = jnp.max(rest, axis=-1)
    g_sel = jnp.argmax(v1 + v2, axis=-1).astype(jnp.int32)
    pick = lambda a: jnp.take_along_axis(a, g_sel[:, None], axis=1)[:, 0]
    sel_v = jnp.stack([pick(v1), pick(v2)], axis=-1)
    sel_i = jnp.stack([pick(i1), pick(i2)], axis=-1)
    expert = g_sel[:, None] * GROUP_SIZE + sel_i
    gate = jax.nn.softmax(sel_v, axis=-1)

    A = N * TOP_K
    flat_e = expert.reshape(A)
    onehot = (flat_e[:, None] == jnp.arange(N_EXPERTS, dtype=jnp.int32)[None, :]).astype(F32)
    ch = 256
    oh3 = onehot.reshape(A // ch, ch, N_EXPERTS)
    tri = (jnp.arange(ch)[:, None] > jnp.arange(ch)[None, :]).astype(F32)
    local = jnp.einsum('ij,bjk->bik', tri, oh3, precision=lax.Precision.HIGHEST)
    chunk_sum = oh3.sum(axis=1)
    chunk_off = jnp.cumsum(chunk_sum, axis=0) - chunk_sum
    rank = jnp.sum((local + chunk_off[:, None, :]) * oh3, axis=-1).reshape(A).astype(jnp.int32)
    counts = chunk_sum.sum(axis=0).astype(jnp.int32)
    padded = ((counts + MOE_BLOCK - 1) // MOE_BLOCK) * MOE_BLOCK
    pad_end = jnp.cumsum(padded)
    pad_start = pad_end - padded
    dest = (jnp.sum(onehot * pad_start.astype(F32)[None, :], axis=-1).astype(jnp.int32) + rank)
    P = n_blocks * MOE_BLOCK
    flat_tok = jnp.arange(A, dtype=jnp.int32) // TOP_K
    slot_tok = jnp.zeros((P,), jnp.int32).at[dest].set(flat_tok)
    blk_start = jnp.arange(n_blocks, dtype=jnp.int32) * MOE_BLOCK
    block_e = jnp.minimum(jnp.sum((pad_end[None, :] <= blk_start[:, None]).astype(jnp.int32), axis=1),
                          N_EXPERTS - 1).astype(jnp.int32)
    n_active = (pad_end[-1] // MOE_BLOCK).astype(jnp.int32).reshape(1)
    return slot_tok, gate, block_e, n_active, dest.reshape(N, TOP_K)


def _row_copy(src, dst, s, d, sem):
    return pltpu.make_async_copy(src.at[pl.ds(s, 1)], dst.at[pl.ds(d, 1)], sem)


def _issue_rows(idx_ref, n_rows, src_ref, dst_ref, sem, unroll=8):
    def issue(c, carry):
        for u in range(unroll):
            r = c * unroll + u
            _row_copy(src_ref, dst_ref, idx_ref[0, 0, r], r, sem).start()
        return carry
    lax.fori_loop(0, n_rows // unroll, issue, 0)


def _wait_rows(n_rows, src_ref, dst_ref, sem):
    pltpu.make_async_copy(src_ref.at[pl.ds(0, n_rows)], dst_ref, sem).wait()


def _gather_kernel(nact_ref, idx_ref, idx_next_ref, src_ref, dst_ref, buf, sems):
    i = pl.program_id(0)
    nact = nact_ref[0]
    slot = i % 2

    @pl.when(i == 0)
    def _():
        _issue_rows(idx_ref, MOE_BLOCK, src_ref, buf.at[0], sems.at[0])

    @pl.when(i + 1 < nact)
    def _():
        _issue_rows(idx_next_ref, MOE_BLOCK, src_ref, buf.at[1 - slot], sems.at[1 - slot])

    @pl.when(i < nact)
    def _():
        _wait_rows(MOE_BLOCK, src_ref, buf.at[slot], sems.at[slot])
        dst_ref[...] = buf[slot].astype(dst_ref.dtype)

    @pl.when(i >= nact)
    def _():
        dst_ref[...] = jnp.zeros(dst_ref.shape, dst_ref.dtype)


def gather_rows(src, slot_tok, n_active, n_blocks):
    D = src.shape[1]
    last = n_blocks - 1
    grid_spec = pltpu.PrefetchScalarGridSpec(
        num_scalar_prefetch=1, grid=(n_blocks,),
        in_specs=[
            pl.BlockSpec((1, 1, MOE_BLOCK), lambda i, na: (i, 0, 0), memory_space=pltpu.SMEM),
            pl.BlockSpec((1, 1, MOE_BLOCK), lambda i, na: (jnp.minimum(i + 1, last), 0, 0),
                         memory_space=pltpu.SMEM),
            pl.BlockSpec(memory_space=pl.ANY),
        ],
        out_specs=pl.BlockSpec((MOE_BLOCK, D), lambda i, na: (i, 0)),
        scratch_shapes=[pltpu.VMEM((2, MOE_BLOCK, D), src.dtype), pltpu.SemaphoreType.DMA((2,))],
    )
    idx3 = slot_tok.reshape(n_blocks, 1, MOE_BLOCK)
    return pl.pallas_call(
        _gather_kernel, grid_spec=grid_spec,
        out_shape=jax.ShapeDtypeStruct((n_blocks * MOE_BLOCK, D), BF16),
        compiler_params=_cparams(1), name="moe_gather")(n_active, idx3, idx3, src)


def _new_expert(be_ref, i):
    return jnp.logical_or(i == 0, be_ref[i] != be_ref[jnp.maximum(i - 1, 0)])


def _moe_up_kernel(be_ref, nact_ref, xs_ref, wg_ref, wu_ref, h_ref, wg_bf, wu_bf):
    i = pl.program_id(1)

    @pl.when(_new_expert(be_ref, i))
    def _():
        _cast_weight(wg_ref, wg_bf)
        _cast_weight(wu_ref, wu_bf)

    @pl.when(i < nact_ref[0])
    def _():
        x = xs_ref[...]
        hg = jnp.dot(x, wg_bf[...], preferred_element_type=F32)
        hu = jnp.dot(x, wu_bf[...], preferred_element_type=F32)
        h_ref[...] = (_silu(hg) * hu).astype(h_ref.dtype)

    @pl.when(i >= nact_ref[0])
    def _():
        h_ref[...] = jnp.zeros(h_ref.shape, h_ref.dtype)


def moe_up(xs, w_up_all, layer, block_e, n_active, fc=512):
    P, D = xs.shape
    F = w_up_all.shape[-1] // 2
    n_blocks = P // MOE_BLOCK
    nfc = F // fc
    grid_spec = pltpu.PrefetchScalarGridSpec(
        num_scalar_prefetch=2, grid=(nfc, n_blocks),
        in_specs=[
            pl.BlockSpec((MOE_BLOCK, D), lambda j, i, be, na: (i, 0)),
            pl.BlockSpec((None, None, D, fc), lambda j, i, be, na: (layer, be[i], 0, j)),
            pl.BlockSpec((None, None, D, fc), lambda j, i, be, na: (layer, be[i], 0, nfc + j)),
        ],
        out_specs=pl.BlockSpec((MOE_BLOCK, fc), lambda j, i, be, na: (i, j)),
        scratch_shapes=[pltpu.VMEM((D, fc), BF16), pltpu.VMEM((D, fc), BF16)],
    )
    return pl.pallas_call(
        _moe_up_kernel, grid_spec=grid_spec,
        out_shape=jax.ShapeDtypeStruct((P, F), BF16),
        compiler_params=_cparams(2), name="moe_up")(block_e, n_active, xs, w_up_all, w_up_all)


def _moe_down_kernel(be_ref, nact_ref, h_ref, wd_ref, y_ref, wd_bf):
    i = pl.program_id(1)

    @pl.when(_new_expert(be_ref, i))
    def _():
        _cast_weight(wd_ref, wd_bf)

    @pl.when(i < nact_ref[0])
    def _():
        y_ref[...] = jnp.dot(h_ref[...], wd_bf[...], preferred_element_type=F32)

    @pl.when(i >= nact_ref[0])
    def _():
        y_ref[...] = jnp.zeros(y_ref.shape, y_ref.dtype)


def moe_down(h, w_down_all, layer, block_e, n_active, tn=2048):
    P, F = h.shape
    D = w_down_all.shape[-1]
    n_blocks = P // MOE_BLOCK
    grid_spec = pltpu.PrefetchScalarGridSpec(
        num_scalar_prefetch=2, grid=(D // tn, n_blocks),
        in_specs=[
            pl.BlockSpec((MOE_BLOCK, F), lambda j, i, be, na: (i, 0)),
            pl.BlockSpec((None, None, F, tn), lambda j, i, be, na: (layer, be[i], 0, j)),
        ],
        out_specs=pl.BlockSpec((MOE_BLOCK, tn), lambda j, i, be, na: (i, j)),
        scratch_shapes=[pltpu.VMEM((F, tn), BF16)],
    )
    return pl.pallas_call(
        _moe_down_kernel, grid_spec=grid_spec,
        out_shape=jax.ShapeDtypeStruct((P, D), F32),
        compiler_params=_cparams(2), name="moe_down")(block_e, n_active, h, w_down_all)


def _combine_kernel(pos_ref, pos_next_ref, x_ref, g_ref, gate_ref, y_ref, gn_ref, *rest,
                    tm, n_tiles, final):
    buf, sems = rest[-2:]
    i = pl.program_id(0)
    slot = i % 2

    def issue(idx_ref, s, unroll=4):
        def body(c, carry):
            for u in range(unroll):
                r = c * unroll + u
                for k in range(TOP_K):
                    _row_copy(y_ref, buf.at[s, k], idx_ref[0, 0, TOP_K * r + k], r,
                              sems.at[s]).start()
            return carry
        lax.fori_loop(0, tm // unroll, body, 0)

    @pl.when(i == 0)
    def _():
        issue(pos_ref, 0)

    @pl.when(i + 1 < n_tiles)
    def _():
        issue(pos_next_ref, 1 - slot)

    for k in range(TOP_K):
        _wait_rows(tm, y_ref, buf.at[slot, k], sems.at[slot])
    gate = gate_ref[...]
    moe = buf[slot, 0] * gate[:, 0:1] + buf[slot, 1] * gate[:, 1:2]
    xn = x_ref[...] + g_ref[0] * moe
    yn = _rms(xn, gn_ref[...])
    if final:
        rest[0][...] = yn
    else:
        sc_ref, sh_ref, o_ref, u_ref = rest[:4]
        o_ref[...] = xn
        u_ref[...] = (yn * (1.0 + sc_ref[0]) + sh_ref[0]).astype(u_ref.dtype)


def moe_combine(x, y, pos, gate, mod3, g_chunk, row_fn, g_next, mod3_next=None, tm=128):
    M, D = x.shape
    n_tiles = M // tm
    final = mod3_next is None
    pos3 = pos.reshape(n_tiles, 1, TOP_K * tm)
    row_spec = pl.BlockSpec((tm, D), lambda i: (i, 0))
    in_specs = [
        pl.BlockSpec((1, 1, TOP_K * tm), lambda i: (i, 0, 0), memory_space=pltpu.SMEM),
        pl.BlockSpec((1, 1, TOP_K * tm), lambda i: (jnp.minimum(i + 1, n_tiles - 1), 0, 0),
                     memory_space=pltpu.SMEM),
        row_spec,
        pl.BlockSpec((1, 1, D), lambda i: (row_fn(i), 0, g_chunk)),
        pl.BlockSpec((tm, TOP_K), lambda i: (i, 0)),
        pl.BlockSpec(memory_space=pl.ANY),
        pl.BlockSpec((1, D), lambda i: (0, 0)),
    ]
    args = [pos3, pos3, x, mod3, gate, y, g_next.reshape(1, D)]
    scratch = [pltpu.VMEM((2, TOP_K, tm, D), F32), pltpu.SemaphoreType.DMA((2,))]
    kern = functools.partial(_combine_kernel, tm=tm, n_tiles=n_tiles, final=final)
    if final:
        return pl.pallas_call(
            kern, grid=(n_tiles,), in_specs=in_specs, out_specs=row_spec,
            out_shape=jax.ShapeDtypeStruct((M, D), F32), scratch_shapes=scratch,
            compiler_params=_cparams(1), name="moe_combine_final")(*args)
    in_specs += [pl.BlockSpec((1, 1, D), lambda i: (row_fn(i), 0, 1)),
                 pl.BlockSpec((1, 1, D), lambda i: (row_fn(i), 0, 0))]
    return pl.pallas_call(
        kern, grid=(n_tiles,), in_specs=in_specs, out_specs=[row_spec, row_spec],
        out_shape=[jax.ShapeDtypeStruct((M, D), F32), jax.ShapeDtypeStruct((M, D), BF16)],
        scratch_shapes=scratch, input_output_aliases={2: 0},
        compiler_params=_cparams(1), name="moe_combine")(*args, mod3_next, mod3_next)


def kernel(x_prompt, x_sample, cache_k, cache_v, state_rglru, c, c_ctx, w_ada, b_ada, norm_attn,
           norm_ffn, norm_final, w_in, w_pool, pool_scale, rpb, conv_w, conv_b, rg_w_a, rg_b_a,
           rg_w_i, rg_b_i, rg_lam, w_out, w_router, b_router, w_up, w_down):
    B, S, D = x_prompt.shape
    DB, T, _ = x_sample.shape
    L = w_in.shape[0]
    pool_w = w_pool.shape[1] * w_pool.shape[2]
    na_w = cache_k.shape[3] * cache_k.shape[4]
    rg_w = rg_lam.shape[-1]
    n_ctx = B * S
    n_tok = n_ctx + DB * T
    q_col0 = pool_w
    x_col0 = pool_w + 3 * na_w

    n_cond = 8
    cond = jnp.concatenate([c_ctx[None, :], c, jnp.zeros((n_cond - 1 - DB, D), F32)], axis=0)
    mod = ada_mod(cond, w_ada, b_ada)

    x = jnp.concatenate([x_prompt.reshape(n_ctx, D), x_sample.reshape(DB * T, D)], axis=0)

    wr = jnp.zeros((D, 128), F32).at[:, :N_EXPERTS].set(w_router).astype(BF16)
    br = jnp.zeros((1, 128), F32).at[0, :N_EXPERTS].set(b_router)
    n_blocks = (n_tok * TOP_K) // MOE_BLOCK + N_EXPERTS

    ck4 = cache_k.reshape(DB, L, cache_k.shape[2], na_w)
    cv4 = cache_v.reshape(DB, L, cache_v.shape[2], na_w)
    rpb_flat = rpb.reshape(L, -1)
    rg_params = dict(conv_w=conv_w, conv_b=conv_b, rg_w_a=rg_w_a, rg_w_i=rg_w_i, rg_b_a=rg_b_a,
                     rg_b_i=rg_b_i, rg_lam=rg_lam)
    zeros_h = jnp.zeros((B, 2, rg_w), F32)

    new_k, new_v, new_h = [], [], []
    row256 = _cond_row_fn(256, n_ctx, T)
    row512 = _cond_row_fn(512, n_ctx, T)
    row128 = _cond_row_fn(128, n_ctx, T)
    mod3s = [mod[l].reshape(n_cond, 1, 6 * D) for l in range(L)]
    u = norm_mod(x, norm_attn[0], mod3s[0], 1, 0, row256, BF16)
    for l in range(L):
        mod3 = mod3s[l]
        proj = mm_ws(u, w_in, l, tm=1024, tn=1024)

        kv = proj[:n_ctx, q_col0 + na_w:q_col0 + 3 * na_w]
        new_k.append(kv[:, :na_w].reshape(B, S, -1, NA_HEAD_DIM))
        new_v.append(kv[:, na_w:].reshape(B, S, -1, NA_HEAD_DIM))

        mix = pool_mixer(proj, w_pool, pool_scale, l, None, S, B, 0, D)
        mix = pool_mixer(proj, w_pool, pool_scale, l, mix, T, DB, n_ctx // T, D)
        mix = ctx_attention(proj, mix, S, B, na_w, q_col0)
        mix = natten(proj, ck4, cv4, rpb_flat, l, mix, T, DB, n_ctx // T, q_col0, na_w)
        mix, h_ctx = rglru(proj, rg_params, l, zeros_h, mix, S, B, 0, x_col0, pool_w + na_w)
        mix, _ = rglru(proj, rg_params, l, state_rglru[:, l], mix, T, DB, n_ctx // T, x_col0,
                       pool_w + na_w)
        new_h.append(h_ctx)

        x = mm_ws(mix, w_out, l, tm=512, tn=1024, res=x, gate=(mod3, 2, row512))

        u2, logits = norm_mod(x, norm_ffn[l], mod3, 4, 3, row256, F32, router=(wr, br))
        slot_tok, gate, block_e, n_active, pos = route(logits[:, :N_EXPERTS], n_blocks)
        xs = gather_rows(u2, slot_tok, n_active, n_blocks)
        hmid = moe_up(xs, w_up, l, block_e, n_active)
        y = moe_down(hmid, w_down, l, block_e, n_active)
        if l + 1 < L:
            x, u = moe_combine(x, y, pos, gate, mod3, 5, row128, norm_attn[l + 1], mod3s[l + 1])
        else:
            yf = moe_combine(x, y, pos, gate, mod3, 5, row128, norm_final)

    y_prompt = yf[:n_ctx].reshape(B, S, D)
    y_sample = yf[n_ctx:].reshape(DB, T, D)
    return (y_prompt, y_sample, jnp.stack(new_k, axis=1), jnp.stack(new_v, axis=1),
            jnp.stack(new_h, axis=1))
```

```python
import functools

import jax
import jax.numpy as jnp
from jax import lax
from jax.experimental import pallas as pl
from jax.experimental.pallas import tpu as pltpu

F32 = jnp.float32
BF16 = jnp.bfloat16

EPS = 1e-6
GRID_W = 64
POOL_WINDOWS = (2, 4, 8, 16)
N_POOL_GROUPS = 4
WIN_R = 8
WIN_C = 16
NA_HEAD_DIM = 128
RG_BLOCK_DIM = 128
CONV_W = 4
RG_C = 8.0
N_EXPERTS = 16
N_EXPERT_GROUPS = 4
GROUP_SIZE = N_EXPERTS // N_EXPERT_GROUPS
TOP_K = 2
MOE_BLOCK = 512

NEG_BIG = -1e30
LOG2E = 1.4426950408889634
VMEM_LIMIT_BYTES = 60 * 1024 * 1024

NAT_R = 4
NAT_NK = 12


def _cparams(n_axes):
    return pltpu.CompilerParams(
        dimension_semantics=("arbitrary",) * n_axes, vmem_limit_bytes=VMEM_LIMIT_BYTES)


def _silu(x):
    return x * jax.nn.sigmoid(x)


def _sigmoid_tanh(x):
    return 0.5 * jnp.tanh(0.5 * x) + 0.5


def _ada_kernel(c_ref, w_ref, b_ref, o_ref):
    s = _silu(c_ref[...]).astype(BF16)
    w = w_ref[...].astype(BF16)
    o_ref[...] = jnp.dot(s, w, preferred_element_type=F32) + b_ref[...]


def ada_mod(cond, w_ada, b_ada, tn=512):
    L, D, N = w_ada.shape
    R = cond.shape[0]
    return pl.pallas_call(
        _ada_kernel,
        grid=(L, N // tn),
        in_specs=[
            pl.BlockSpec((R, D), lambda l, j: (0, 0)),
            pl.BlockSpec((None, D, tn), lambda l, j: (l, 0, j)),
            pl.BlockSpec((None, 1, tn), lambda l, j: (l, 0, j)),
        ],
        out_specs=pl.BlockSpec((None, R, tn), lambda l, j: (l, 0, j)),
        out_shape=jax.ShapeDtypeStruct((L, R, N), F32),
        compiler_params=_cparams(2),
        name="ada_mod",
    )(cond, w_ada, b_ada.reshape(L, 1, N))


def _rms(x, g):
    ms = jnp.mean(x * x, axis=-1, keepdims=True)
    return x * lax.rsqrt(ms + EPS) * g


def _norm_mod_router_kernel(x_ref, g_ref, sc_ref, sh_ref, wr_ref, br_ref, u_ref, lg_ref):
    y = _rms(x_ref[...], g_ref[...])
    u = y * (1.0 + sc_ref[0]) + sh_ref[0]
    u_ref[...] = u.astype(u_ref.dtype)
    lg_ref[...] = jnp.dot(u.astype(BF16), wr_ref[...], preferred_element_type=F32) + br_ref[...]


def _norm_mod_first_kernel(xa_ref, xb_ref, g_ref, sc_ref, sh_ref, x_ref, u_ref, *, n_a):
    x = jnp.where(pl.program_id(0) < n_a, xa_ref[...], xb_ref[...])
    x_ref[...] = x
    u_ref[...] = (_rms(x, g_ref[...]) * (1.0 + sc_ref[0]) + sh_ref[0]).astype(u_ref.dtype)


def norm_mod_first(xa, xb, g, mod3, sc_chunk, sh_chunk, row_fn, tm=256):
    D = xa.shape[1]
    n_a, n_b = xa.shape[0] // tm, xb.shape[0] // tm
    M = xa.shape[0] + xb.shape[0]
    row_spec = pl.BlockSpec((tm, D), lambda i: (i, 0))
    return pl.pallas_call(
        functools.partial(_norm_mod_first_kernel, n_a=n_a), grid=(n_a + n_b,),
        in_specs=[
            pl.BlockSpec((tm, D), lambda i: (jnp.minimum(i, n_a - 1), 0)),
            pl.BlockSpec((tm, D), lambda i: (jnp.maximum(i - n_a, 0), 0)),
            pl.BlockSpec((1, D), lambda i: (0, 0)),
            pl.BlockSpec((1, 1, D), lambda i: (row_fn(i), 0, sc_chunk)),
            pl.BlockSpec((1, 1, D), lambda i: (row_fn(i), 0, sh_chunk)),
        ],
        out_specs=[row_spec, row_spec],
        out_shape=[jax.ShapeDtypeStruct((M, D), F32), jax.ShapeDtypeStruct((M, D), BF16)],
        compiler_params=_cparams(1), name="norm_mod_first")(xa, xb, g.reshape(1, D), mod3, mod3)


def _cond_row_fn(tm, n_ctx_rows, dec_seq):
    def row(i):
        r0 = i * tm
        return jnp.where(r0 < n_ctx_rows, 0, 1 + (r0 - n_ctx_rows) // dec_seq)
    return row


def norm_mod_router(x, g, mod3, sc_chunk, sh_chunk, row_fn, out_dtype, router, tm=256):
    M, D = x.shape
    wr, br = router
    NE = wr.shape[1]
    in_specs = [
        pl.BlockSpec((tm, D), lambda i: (i, 0)),
        pl.BlockSpec((1, D), lambda i: (0, 0)),
        pl.BlockSpec((1, 1, D), lambda i: (row_fn(i), 0, sc_chunk)),
        pl.BlockSpec((1, 1, D), lambda i: (row_fn(i), 0, sh_chunk)),
        pl.BlockSpec((D, NE), lambda i: (0, 0)),
        pl.BlockSpec((1, NE), lambda i: (0, 0)),
    ]
    return pl.pallas_call(
        _norm_mod_router_kernel, grid=(M // tm,), in_specs=in_specs,
        out_specs=[pl.BlockSpec((tm, D), lambda i: (i, 0)), pl.BlockSpec((tm, NE), lambda i: (i, 0))],
        out_shape=[jax.ShapeDtypeStruct((M, D), out_dtype), jax.ShapeDtypeStruct((M, NE), F32)],
        compiler_params=_cparams(1), name="norm_mod_router")(x, g.reshape(1, D), mod3, mod3, wr, br)


def _cast_weight(w_ref, wbf_ref, kchunk=512):
    K = w_ref.shape[0]
    for k0 in range(0, K, kchunk):
        k1 = min(K, k0 + kchunk)
        wbf_ref[k0:k1, :] = w_ref[k0:k1, :].astype(BF16)


def _mm_kernel(lhs_ref, w_ref, o_ref, wbf_ref):
    @pl.when(pl.program_id(1) == 0)
    def _():
        _cast_weight(w_ref, wbf_ref)
    o_ref[...] = jnp.dot(lhs_ref[...], wbf_ref[...], preferred_element_type=F32).astype(o_ref.dtype)


def _mm_res_kernel(lhs_ref, w_ref, res_ref, g_ref, o_ref, wbf_ref):
    @pl.when(pl.program_id(1) == 0)
    def _():
        _cast_weight(w_ref, wbf_ref)
    acc = jnp.dot(lhs_ref[...], wbf_ref[...], preferred_element_type=F32)
    o_ref[...] = res_ref[...] + g_ref[0] * acc


def mm_ws(lhs, w_all, layer, tm, tn, res=None, gate=None, out_dtype=F32):
    M, K = lhs.shape
    N = w_all.shape[-1]
    in_specs = [
        pl.BlockSpec((tm, K), lambda j, i: (i, 0)),
        pl.BlockSpec((None, K, tn), lambda j, i: (layer, 0, j), pipeline_mode=pl.Buffered(1)),
    ]
    out_spec = pl.BlockSpec((tm, tn), lambda j, i: (i, j))
    scratch = [pltpu.VMEM((K, tn), BF16)]
    if res is None:
        return pl.pallas_call(
            _mm_kernel, grid=(N // tn, M // tm), in_specs=in_specs, out_specs=out_spec,
            out_shape=jax.ShapeDtypeStruct((M, N), out_dtype), scratch_shapes=scratch,
            compiler_params=_cparams(2), name="mm_ws")(lhs, w_all)
    mod3, g_chunk, row_fn = gate
    in_specs += [
        pl.BlockSpec((tm, tn), lambda j, i: (i, j)),
        pl.BlockSpec((1, 1, tn), lambda j, i: (row_fn(i), 0, g_chunk * (N // tn) + j)),
    ]
    return pl.pallas_call(
        _mm_res_kernel, grid=(N // tn, M // tm), in_specs=in_specs, out_specs=out_spec,
        out_shape=jax.ShapeDtypeStruct((M, N), F32), scratch_shapes=scratch,
        input_output_aliases={2: 0},
        compiler_params=_cparams(2), name="mm_ws_res")(lhs, w_all, res, mod3)


POOL_PAD = 16


def _pool_kernel(x_ref, w_ref, s_ref, *rest, T, CH):
    o_ref, xpad = rest[-2], rest[-1]
    g = pl.program_id(1)
    C = x_ref.shape[1]
    xpad[0:POOL_PAD, :] = jnp.zeros((POOL_PAD, C), F32)
    xpad[T + POOL_PAD:T + 2 * POOL_PAD, :] = jnp.zeros((POOL_PAD, C), F32)
    for c in range(T // CH):
        xpad[POOL_PAD + c * CH:POOL_PAD + (c + 1) * CH, :] = x_ref[c * CH:(c + 1) * CH, :]
    wbf = w_ref[...].astype(BF16)
    scale = s_ref[...]
    for gi, win in enumerate(POOL_WINDOWS):
        half = win // 2

        @pl.when(g == gi)
        def _():
            for c in range(T // CH):
                base = c * CH
                acc = xpad[base + POOL_PAD - half:base + POOL_PAD - half + CH, :]
                for d in range(-half + 1, half):
                    acc = acc + xpad[base + POOL_PAD + d:base + POOL_PAD + d + CH, :]
                t = base + lax.broadcasted_iota(jnp.int32, (CH, 1), 0)
                cnt = (jnp.minimum(t + half, T) - jnp.maximum(t - half, 0)).astype(F32)
                xc = xpad[base + POOL_PAD:base + POOL_PAD + CH, :]
                pooled = (acc / cnt - xc).astype(BF16)
                y = jnp.dot(pooled, wbf, preferred_element_type=F32) * scale
                o_ref[base:base + CH, :] = y.astype(o_ref.dtype)


def pool_mixer(proj, w_pool_all, pool_scale_all, layer, mix, T, n_seq, row_blk0, mix_cols):
    C = w_pool_all.shape[-1]
    W = N_POOL_GROUPS * C
    in_specs = [
        pl.BlockSpec((T, C), lambda b, g: (row_blk0 + b, g)),
        pl.BlockSpec((None, None, C, C), lambda b, g: (layer, g, 0, 0)),
        pl.BlockSpec((None, 1, C), lambda b, g: (layer, 0, g)),
    ]
    args = [proj, w_pool_all, pool_scale_all.reshape(-1, 1, W)]
    aliases = {}
    if mix is not None:
        in_specs.append(pl.BlockSpec(memory_space=pl.ANY))
        args.append(mix)
        aliases = {3: 0}
    return pl.pallas_call(
        functools.partial(_pool_kernel, T=T, CH=min(T, 256)),
        grid=(n_seq, N_POOL_GROUPS), in_specs=in_specs,
        out_specs=pl.BlockSpec((T, C), lambda b, g: (row_blk0 + b, g)),
        out_shape=jax.ShapeDtypeStruct((proj.shape[0], mix_cols), BF16),
        scratch_shapes=[pltpu.VMEM((T + 2 * POOL_PAD, C), F32)],
        input_output_aliases=aliases,
        compiler_params=_cparams(2), name=f"pool_T{T}")(*args)


def _ctx_attn_kernel(q_ref, k_ref, v_ref, mix_ref, o_ref, *, n_heads):
    del mix_ref
    Dh = NA_HEAD_DIM
    scale = Dh ** -0.5
    for h in range(n_heads):
        sl = slice(h * Dh, (h + 1) * Dh)
        q = q_ref[:, sl].astype(BF16)
        k = k_ref[:, sl].astype(BF16)
        v = v_ref[:, sl].astype(BF16)
        s = lax.dot_general(q, k, (((1,), (1,)), ((), ())), preferred_element_type=F32) * scale
        m = jnp.max(s, axis=-1, keepdims=True)
        p = jnp.exp(s - m)
        l = jnp.sum(p, axis=-1, keepdims=True)
        o = jnp.dot(p.astype(BF16), v, preferred_element_type=F32)
        o_ref[:, sl] = (o * (1.0 / l)).astype(o_ref.dtype)


def ctx_attention(proj, mix, S, n_seq, na_width, q_col0):
    cb = 1024
    nhalf = na_width // cb
    qb, kb, vb = q_col0 // cb, (q_col0 + na_width) // cb, (q_col0 + 2 * na_width) // cb
    return pl.pallas_call(
        functools.partial(_ctx_attn_kernel, n_heads=cb // NA_HEAD_DIM),
        grid=(n_seq, nhalf),
        in_specs=[
            pl.BlockSpec((S, cb), lambda b, c: (b, qb + c)),
            pl.BlockSpec((S, cb), lambda b, c: (b, kb + c)),
            pl.BlockSpec((S, cb), lambda b, c: (b, vb + c)),
            pl.BlockSpec(memory_space=pl.ANY),
        ],
        out_specs=pl.BlockSpec((S, cb), lambda b, c: (b, qb + c)),
        out_shape=jax.ShapeDtypeStruct(mix.shape, mix.dtype),
        input_output_aliases={3: 0},
        compiler_params=_cparams(2), name="ctx_attn")(proj, proj, proj, mix)


def _natten_cases(rows):
    R, NK = NAT_R, NAT_NK
    cases = []
    for r0 in (0, R, rows - R):
        start = min(max(r0 - WIN_R // 2, 0), rows - NK)
        tiles = []
        for a in range(R):
            r = r0 + a
            rs = min(max(r - WIN_R // 2, 0), rows - WIN_R)
            row = []
            for i in range(NK):
                kr = start + i
                row.append(kr - r + (WIN_R - 1) if rs <= kr < rs + WIN_R else None)
            tiles.append(row)
        cases.append((start - r0, tiles))
    return cases


def _natten_kernel(rpb_ref, q_ref, k_ref, v_ref, ck_ref, cv_ref, mix_ref, o_ref,
                   qbf, kbf, vbf, ckbf, cvbf, tb, bm, *, rows):
    del mix_ref
    W = GRID_W
    R, NK = NAT_R, NAT_NK
    RQ, NKK = R * W, NK * W
    Dh = NA_HEAD_DIM
    scale2 = Dh ** -0.5 * LOG2E
    h = pl.program_id(1)
    n_dr, n_dc = 2 * WIN_R - 1, 2 * WIN_C - 1
    T = rows * W

    for c0 in range(0, T, 512):
        qbf[c0:c0 + 512, :] = q_ref[c0:c0 + 512, :].astype(BF16)
        kbf[c0:c0 + 512, :] = k_ref[c0:c0 + 512, :].astype(BF16)
        vbf[c0:c0 + 512, :] = v_ref[c0:c0 + 512, :].astype(BF16)
    ckbf[...] = ck_ref[...].astype(BF16)
    cvbf[...] = cv_ref[...].astype(BF16)

    ci = lax.broadcasted_iota(jnp.int32, (W, W), 0)
    kci = lax.broadcasted_iota(jnp.int32, (W, W), 1)
    dci = kci - ci + (WIN_C - 1)
    cs = jnp.clip(ci - WIN_C // 2, 0, W - WIN_C)
    in_win = (kci >= cs) & (kci < cs + WIN_C)
    for dr in range(n_dr):
        t = jnp.full((W, W), NEG_BIG, F32)
        for dc in range(n_dc):
            t = jnp.where(dci == dc, rpb_ref[h * (n_dr * n_dc) + dr * n_dc + dc] * LOG2E, t)
        tb[dr] = jnp.where(in_win, t, NEG_BIG)

    cases = _natten_cases(rows)
    neg_tile = jnp.full((W, W), NEG_BIG, F32)
    for ci_, (_, tiles) in enumerate(cases):
        for a in range(R):
            for i in range(0, NK, 2):
                pair = [tb[d] if d is not None else neg_tile for d in tiles[a][i:i + 2]]
                bm[ci_, a * W:(a + 1) * W, i * W:(i + 2) * W] = jnp.concatenate(pair, axis=1)

    nt = (((1,), (1,)), ((), ()))

    def block(q0, k0, case):
        qb = qbf[pl.ds(q0, RQ), :]
        kk = kbf[pl.ds(k0, NKK), :]
        vv = vbf[pl.ds(k0, NKK), :]
        s1 = lax.dot_general(qb, kk, nt, preferred_element_type=F32) * scale2 + bm[case]
        s2 = lax.dot_general(qb, ckbf[...], nt, preferred_element_type=F32) * scale2
        m = jnp.maximum(jnp.max(s1, axis=-1, keepdims=True), jnp.max(s2, axis=-1, keepdims=True))
        p1 = jnp.exp2(s1 - m)
        p2 = jnp.exp2(s2 - m)
        l = jnp.sum(p1, axis=-1, keepdims=True) + jnp.sum(p2, axis=-1, keepdims=True)
        o = (jnp.dot(p1.astype(BF16), vv, preferred_element_type=F32)
             + jnp.dot(p2.astype(BF16), cvbf[...], preferred_element_type=F32))
        o_ref[pl.ds(q0, RQ), :] = (o * (1.0 / l)).astype(o_ref.dtype)

    block(0, (cases[0][0]) * W, 0)

    for blk in range(1, rows // R - 1):
        block(blk * RQ, (blk * R + cases[1][0]) * W, 1)
    q_last = (rows - R) * W
    block(q_last, q_last + cases[2][0] * W, 2)


def natten(proj, cache_k4, cache_v4, rpb_all, layer, mix, T, n_seq, row_blk0, q_col0, na_width):
    Dh = NA_HEAD_DIM
    H = na_width // Dh
    P = cache_k4.shape[2]
    rows = T // GRID_W
    qb, kb, vb = q_col0 // Dh, (q_col0 + na_width) // Dh, (q_col0 + 2 * na_width) // Dh
    return pl.pallas_call(
        functools.partial(_natten_kernel, rows=rows),
        grid=(n_seq, H),
        in_specs=[
            pl.BlockSpec(memory_space=pltpu.SMEM),
            pl.BlockSpec((T, Dh), lambda b, h: (row_blk0 + b, qb + h)),
            pl.BlockSpec((T, Dh), lambda b, h: (row_blk0 + b, kb + h)),
            pl.BlockSpec((T, Dh), lambda b, h: (row_blk0 + b, vb + h)),
            pl.BlockSpec((None, None, P, Dh), lambda b, h: (b, layer, 0, h)),
            pl.BlockSpec((None, None, P, Dh), lambda b, h: (b, layer, 0, h)),
            pl.BlockSpec(memory_space=pl.ANY),
        ],
        out_specs=pl.BlockSpec((T, Dh), lambda b, h: (row_blk0 + b, qb + h)),
        out_shape=jax.ShapeDtypeStruct(mix.shape, mix.dtype),
        scratch_shapes=[
            pltpu.VMEM((T, Dh), BF16), pltpu.VMEM((T, Dh), BF16), pltpu.VMEM((T, Dh), BF16),
            pltpu.VMEM((P, Dh), BF16), pltpu.VMEM((P, Dh), BF16),
            pltpu.VMEM((2 * WIN_R - 1, GRID_W, GRID_W), F32),
            pltpu.VMEM((3, NAT_R * GRID_W, NAT_NK * GRID_W), F32),
        ],
        input_output_aliases={6: 0},
        compiler_params=_cparams(2), name="natten")(
            rpb_all[layer], proj, proj, proj, cache_k4, cache_v4, mix)


RG_PAD = 8
SCAN_ROWS = 8


def _scan8(a, b, reverse):
    row = lax.broadcasted_iota(jnp.int32, a.shape, 0)
    for s in (1, 2, 4):
        if reverse:
            sh, keep = SCAN_ROWS - s, row < SCAN_ROWS - s
        else:
            sh, keep = s, row >= s
        a_s = pltpu.roll(a, sh, 0)
        b_s = pltpu.roll(b, sh, 0)
        b = jnp.where(keep, a * b_s + b, b)
        a = jnp.where(keep, a * a_s, a)
    return a, b


def _gelu_tanh(x):
    return 0.5 * x * (1.0 + jnp.tanh(0.7978845608028654 * (x + 0.044715 * (x * x * x))))


def _rglru_kernel(x_ref, gt_ref, cw_ref, cb_ref, wa_ref, wi_ref, ba_ref, bi_ref, lam_ref, h0_ref,
                  *rest, T, CH, U):
    y_ref, hs_ref, xpad, a_f, b_f, a_b, b_b, hf_s = rest[-8:]
    C = x_ref.shape[1]
    xpad[0:RG_PAD, :] = jnp.zeros((RG_PAD, C), F32)
    xpad[T + RG_PAD:T + 2 * RG_PAD, :] = jnp.zeros((RG_PAD, C), F32)
    for c in range(T // CH):
        xpad[RG_PAD + c * CH:RG_PAD + (c + 1) * CH, :] = x_ref[c * CH:(c + 1) * CH, :]

    wcat = jnp.concatenate([wa_ref[0], wa_ref[1], wi_ref[0], wi_ref[1]], axis=1).astype(BF16)
    bcat = jnp.concatenate([ba_ref[0], ba_ref[1], bi_ref[0], bi_ref[1]], axis=1)
    z = -lam_ref[...]
    sp = jnp.maximum(z, 0.0) + jnp.log1p(jnp.exp(-jnp.abs(z)))
    cw = cw_ref[...]
    cb = cb_ref[...]
    lead = RG_PAD - CONV_W // 2

    def gates(c, carry):
        c0 = pl.multiple_of(c * CH, CH)
        win = xpad[pl.ds(c0, CH + 2 * RG_PAD), :]
        xc = cb
        for k in range(CONV_W):
            xc = xc + win[lead + k:lead + k + CH, :] * cw[k:k + 1, :]
        gz = jnp.dot(xc.astype(BF16), wcat, preferred_element_type=F32) + bcat
        for d, (a_s, b_s) in enumerate(((a_f, b_f), (a_b, b_b))):
            r = _sigmoid_tanh(gz[:, d * C:(d + 1) * C])
            ig = _sigmoid_tanh(gz[:, (2 + d) * C:(3 + d) * C])
            log_a = -RG_C * r * sp[d]
            a = jnp.exp(log_a)
            a_s[pl.ds(c0, CH), :] = a
            b_s[pl.ds(c0, CH), :] = jnp.sqrt(1.0 - a * a) * ig * xc
        return carry

    lax.fori_loop(0, T // CH, gates, 0)

    n_it = T // (SCAN_ROWS * U)

    def fwd(j, h):
        r0 = pl.multiple_of(j * (SCAN_ROWS * U), SCAN_ROWS * U)
        av = a_f[pl.ds(r0, SCAN_ROWS * U), :]
        bv = b_f[pl.ds(r0, SCAN_ROWS * U), :]
        outs = []
        for u in range(U):
            a8, b8 = _scan8(av[u * 8:(u + 1) * 8], bv[u * 8:(u + 1) * 8], False)
            hrows = b8 + a8 * h
            outs.append(hrows)
            h = jnp.broadcast_to(hrows[SCAN_ROWS - 1:SCAN_ROWS, :], hrows.shape)
        hf_s[pl.ds(r0, SCAN_ROWS * U), :] = jnp.concatenate(outs, axis=0)
        return h

    h0 = h0_ref[...]
    hf_last = lax.fori_loop(0, n_it, fwd, jnp.broadcast_to(h0[0:1, :], (SCAN_ROWS, C)))

    def bwd(jj, h):
        j = n_it - 1 - jj
        r0 = pl.multiple_of(j * (SCAN_ROWS * U), SCAN_ROWS * U)
        av = a_b[pl.ds(r0, SCAN_ROWS * U), :]
        bv = b_b[pl.ds(r0, SCAN_ROWS * U), :]
        outs = [None] * U
        for u in reversed(range(U)):
            a8, b8 = _scan8(av[u * 8:(u + 1) * 8], bv[u * 8:(u + 1) * 8], True)
            hrows = b8 + a8 * h
            outs[u] = hrows
            h = jnp.broadcast_to(hrows[0:1, :], hrows.shape)
        hb = jnp.concatenate(outs, axis=0)
        hf = hf_s[pl.ds(r0, SCAN_ROWS * U), :]
        gate = gt_ref[pl.ds(r0, SCAN_ROWS * U), :]
        y_ref[pl.ds(r0, SCAN_ROWS * U), :] = ((hf + hb) * _gelu_tanh(gate)).astype(y_ref.dtype)
        return h

    hb_first = lax.fori_loop(0, n_it, bwd, jnp.broadcast_to(h0[1:2, :], (SCAN_ROWS, C)))
    hs_ref[0:1, :] = hf_last[SCAN_ROWS - 1:SCAN_ROWS, :]
    hs_ref[1:2, :] = hb_first[0:1, :]


def rglru(proj, p, layer, h0, mix, T, n_seq, row_blk0, x_col0, mix_col0):
    C = RG_BLOCK_DIM
    W = h0.shape[-1]
    nb = W // C
    xb, gb, mb = x_col0 // C, (x_col0 + W) // C, mix_col0 // C
    L = p['conv_w'].shape[0]
    vec = lambda arr: arr.reshape(L, -1, 1, W)
    in_specs = [
        pl.BlockSpec((T, C), lambda b, n: (row_blk0 + b, xb + n)),
        pl.BlockSpec((T, C), lambda b, n: (row_blk0 + b, gb + n)),
        pl.BlockSpec((None, CONV_W, C), lambda b, n: (layer, 0, n)),
        pl.BlockSpec((None, 1, C), lambda b, n: (layer, 0, n)),
        pl.BlockSpec((None, 2, None, C, C), lambda b, n: (layer, 0, n, 0, 0)),
        pl.BlockSpec((None, 2, None, C, C), lambda b, n: (layer, 0, n, 0, 0)),
        pl.BlockSpec((None, 2, 1, C), lambda b, n: (layer, 0, 0, n)),
        pl.BlockSpec((None, 2, 1, C), lambda b, n: (layer, 0, 0, n)),
        pl.BlockSpec((None, 2, 1, C), lambda b, n: (layer, 0, 0, n)),
        pl.BlockSpec((None, 2, C), lambda b, n: (b, 0, n)),
        pl.BlockSpec(memory_space=pl.ANY),
    ]
    args = [proj, proj, p['conv_w'], p['conv_b'].reshape(L, 1, W), p['rg_w_a'], p['rg_w_i'],
            vec(p['rg_b_a']), vec(p['rg_b_i']), vec(p['rg_lam']), h0, mix]
    return pl.pallas_call(
        functools.partial(_rglru_kernel, T=T, CH=min(T, 256), U=4),
        grid=(n_seq, nb), in_specs=in_specs,
        out_specs=[
            pl.BlockSpec((T, C), lambda b, n: (row_blk0 + b, mb + n)),
            pl.BlockSpec((None, 2, C), lambda b, n: (b, 0, n)),
        ],
        out_shape=[jax.ShapeDtypeStruct(mix.shape, mix.dtype),
                   jax.ShapeDtypeStruct((n_seq, 2, W), F32)],
        scratch_shapes=[pltpu.VMEM((T + 2 * RG_PAD, C), F32)] + [pltpu.VMEM((T, C), F32)] * 5,
        input_output_aliases={10: 0},
        compiler_params=_cparams(2), name=f"rglru_T{T}")(*args)


def route(logits, n_blocks):
    N = logits.shape[0]
    lg = logits.reshape(N, N_EXPERT_GROUPS, GROUP_SIZE)
    lane = jnp.arange(GROUP_SIZE, dtype=jnp.int32)
    i1 = jnp.argmax(lg, axis=-1).astype(jnp.int32)
    v1 = jnp.max(lg, axis=-1)
    rest = jnp.where(lane == i1[..., None], -jnp.inf, lg)
    i2 = jnp.argmax(rest, axis=-1).astype(jnp.int32)
    v2 = jnp.max(rest, axis=-1)
    g_sel = jnp.argmax(v1 + v2, axis=-1).astype(jnp.int32)
    pick = lambda a: jnp.take_along_axis(a, g_sel[:, None], axis=1)[:, 0]
    sel_v = jnp.stack([pick(v1), pick(v2)], axis=-1)
    sel_i = jnp.stack([pick(i1), pick(i2)], axis=-1)
    expert = g_sel[:, None] * GROUP_SIZE + sel_i
    gate = jax.nn.softmax(sel_v, axis=-1)

    A = N * TOP_K
    flat_e = expert.reshape(A)
    onehot = (flat_e[:, None] == jnp.arange(N_EXPERTS, dtype=jnp.int32)[None, :]).astype(F32)
    ch = 256
    oh3 = onehot.reshape(A // ch, ch, N_EXPERTS)
    tri = (jnp.arange(ch)[:, None] > jnp.arange(ch)[None, :]).astype(F32)
    local = jnp.einsum('ij,bjk->bik', tri, oh3, precision=lax.Precision.HIGHEST)
    chunk_sum = oh3.sum(axis=1)
    chunk_off = jnp.cumsum(chunk_sum, axis=0) - chunk_sum
    rank = jnp.sum((local + chunk_off[:, None, :]) * oh3, axis=-1).reshape(A).astype(jnp.int32)
    counts = chunk_sum.sum(axis=0).astype(jnp.int32)
    padded = ((counts + MOE_BLOCK - 1) // MOE_BLOCK) * MOE_BLOCK
    pad_end = jnp.cumsum(padded)
    pad_start = pad_end - padded
    dest = (jnp.sum(onehot * pad_start.astype(F32)[None, :], axis=-1).astype(jnp.int32) + rank)
    P = n_blocks * MOE_BLOCK
    flat_tok = jnp.arange(A, dtype=jnp.int32) // TOP_K
    slot_tok = jnp.zeros((P,), jnp.int32).at[dest].set(flat_tok)
    blk_start = jnp.arange(n_blocks, dtype=jnp.int32) * MOE_BLOCK
    block_e = jnp.minimum(jnp.sum((pad_end[None, :] <= blk_start[:, None]).astype(jnp.int32), axis=1),
                          N_EXPERTS - 1).astype(jnp.int32)
    n_active = (pad_end[-1] // MOE_BLOCK).astype(jnp.int32).reshape(1)
    return slot_tok, gate, block_e, n_active, dest.reshape(N, TOP_K)


def _row_copy(src, dst, s, d, sem):
    return pltpu.make_async_copy(src.at[pl.ds(s, 1)], dst.at[pl.ds(d, 1)], sem)


def _issue_rows(idx_ref, n_rows, src_ref, dst_ref, sem, unroll=8):
    def issue(c, carry):
        for u in range(unroll):
            r = c * unroll + u
            _row_copy(src_ref, dst_ref, idx_ref[0, 0, r], r, sem).start()
        return carry
    lax.fori_loop(0, n_rows // unroll, issue, 0)


def _wait_rows(n_rows, src_ref, dst_ref, sem):
    pltpu.make_async_copy(src_ref.at[pl.ds(0, n_rows)], dst_ref, sem).wait()


def _gather_kernel(nact_ref, idx_ref, idx_next_ref, src_ref, dst_ref, buf, sems):
    i = pl.program_id(0)
    nact = nact_ref[0]
    slot = i % 2

    @pl.when(i == 0)
    def _():
        _issue_rows(idx_ref, MOE_BLOCK, src_ref, buf.at[0], sems.at[0])

    @pl.when(i + 1 < nact)
    def _():
        _issue_rows(idx_next_ref, MOE_BLOCK, src_ref, buf.at[1 - slot], sems.at[1 - slot])

    @pl.when(i < nact)
    def _():
        _wait_rows(MOE_BLOCK, src_ref, buf.at[slot], sems.at[slot])
        dst_ref[...] = buf[slot].astype(dst_ref.dtype)

    @pl.when(i >= nact)
    def _():
        dst_ref[...] = jnp.zeros(dst_ref.shape, dst_ref.dtype)


def gather_rows(src, slot_tok, n_active, n_blocks):
    D = src.shape[1]
    last = n_blocks - 1
    grid_spec = pltpu.PrefetchScalarGridSpec(
        num_scalar_prefetch=1, grid=(n_blocks,),
        in_specs=[
            pl.BlockSpec((1, 1, MOE_BLOCK), lambda i, na: (i, 0, 0), memory_space=pltpu.SMEM),
            pl.BlockSpec((1, 1, MOE_BLOCK), lambda i, na: (jnp.minimum(i + 1, last), 0, 0),
                         memory_space=pltpu.SMEM),
            pl.BlockSpec(memory_space=pl.ANY),
        ],
        out_specs=pl.BlockSpec((MOE_BLOCK, D), lambda i, na: (i, 0)),
        scratch_shapes=[pltpu.VMEM((2, MOE_BLOCK, D), src.dtype), pltpu.SemaphoreType.DMA((2,))],
    )
    idx3 = slot_tok.reshape(n_blocks, 1, MOE_BLOCK)
    return pl.pallas_call(
        _gather_kernel, grid_spec=grid_spec,
        out_shape=jax.ShapeDtypeStruct((n_blocks * MOE_BLOCK, D), BF16),
        compiler_params=_cparams(1), name="moe_gather")(n_active, idx3, idx3, src)


def _new_expert(be_ref, i):
    return jnp.logical_or(i == 0, be_ref[i] != be_ref[jnp.maximum(i - 1, 0)])


def _moe_up_kernel(be_ref, nact_ref, xs_ref, wg_ref, wu_ref, h_ref, wg_bf, wu_bf):
    i = pl.program_id(1)

    @pl.when(_new_expert(be_ref, i))
    def _():
        _cast_weight(wg_ref, wg_bf)
        _cast_weight(wu_ref, wu_bf)

    @pl.when(i < nact_ref[0])
    def _():
        x = xs_ref[...]
        hg = jnp.dot(x, wg_bf[...], preferred_element_type=F32)
        hu = jnp.dot(x, wu_bf[...], preferred_element_type=F32)
        h_ref[...] = (_silu(hg) * hu).astype(h_ref.dtype)

    @pl.when(i >= nact_ref[0])
    def _():
        h_ref[...] = jnp.zeros(h_ref.shape, h_ref.dtype)


def moe_up(xs, w_up_all, layer, block_e, n_active, fc=512):
    P, D = xs.shape
    F = w_up_all.shape[-1] // 2
    n_blocks = P // MOE_BLOCK
    nfc = F // fc
    grid_spec = pltpu.PrefetchScalarGridSpec(
        num_scalar_prefetch=2, grid=(nfc, n_blocks),
        in_specs=[
            pl.BlockSpec((MOE_BLOCK, D), lambda j, i, be, na: (i, 0)),
            pl.BlockSpec((None, None, D, fc), lambda j, i, be, na: (layer, be[i], 0, j)),
            pl.BlockSpec((None, None, D, fc), lambda j, i, be, na: (layer, be[i], 0, nfc + j)),
        ],
        out_specs=pl.BlockSpec((MOE_BLOCK, fc), lambda j, i, be, na: (i, j)),
        scratch_shapes=[pltpu.VMEM((D, fc), BF16), pltpu.VMEM((D, fc), BF16)],
    )
    return pl.pallas_call(
        _moe_up_kernel, grid_spec=grid_spec,
        out_shape=jax.ShapeDtypeStruct((P, F), BF16),
        compiler_params=_cparams(2), name="moe_up")(block_e, n_active, xs, w_up_all, w_up_all)


def _moe_down_kernel(be_ref, nact_ref, h_ref, wd_ref, y_ref, wd_bf):
    i = pl.program_id(1)

    @pl.when(_new_expert(be_ref, i))
    def _():
        _cast_weight(wd_ref, wd_bf)

    @pl.when(i < nact_ref[0])
    def _():
        y_ref[...] = jnp.dot(h_ref[...], wd_bf[...], preferred_element_type=F32)

    @pl.when(i >= nact_ref[0])
    def _():
        y_ref[...] = jnp.zeros(y_ref.shape, y_ref.dtype)


def moe_down(h, w_down_all, layer, block_e, n_active, tn=2048):
    P, F = h.shape
    D = w_down_all.shape[-1]
    n_blocks = P // MOE_BLOCK
    grid_spec = pltpu.PrefetchScalarGridSpec(
        num_scalar_prefetch=2, grid=(D // tn, n_blocks),
        in_specs=[
            pl.BlockSpec((MOE_BLOCK, F), lambda j, i, be, na: (i, 0)),
            pl.BlockSpec((None, None, F, tn), lambda j, i, be, na: (layer, be[i], 0, j)),
        ],
        out_specs=pl.BlockSpec((MOE_BLOCK, tn), lambda j, i, be, na: (i, j)),
        scratch_shapes=[pltpu.VMEM((F, tn), BF16)],
    )
    return pl.pallas_call(
        _moe_down_kernel, grid_spec=grid_spec,
        out_shape=jax.ShapeDtypeStruct((P, D), F32),
        compiler_params=_cparams(2), name="moe_down")(block_e, n_active, h, w_down_all)


def _combine_kernel(pos_ref, pos_next_ref, x_ref, g_ref, gate_ref, y_ref, gn_ref, *rest,
                    tm, n_tiles, final, n_split):
    buf, sems = rest[-2:]
    i = pl.program_id(0)
    slot = i % 2

    def issue(idx_ref, s, unroll=4):
        def body(c, carry):
            for u in range(unroll):
                r = c * unroll + u
                for k in range(TOP_K):
                    _row_copy(y_ref, buf.at[s, k], idx_ref[0, 0, TOP_K * r + k], r,
                              sems.at[s]).start()
            return carry
        lax.fori_loop(0, tm // unroll, body, 0)

    @pl.when(i == 0)
    def _():
        issue(pos_ref, 0)

    @pl.when(i + 1 < n_tiles)
    def _():
        issue(pos_next_ref, 1 - slot)

    for k in range(TOP_K):
        _wait_rows(tm, y_ref, buf.at[slot, k], sems.at[slot])
    gate = gate_ref[...]
    moe = buf[slot, 0] * gate[:, 0:1] + buf[slot, 1] * gate[:, 1:2]
    xn = x_ref[...] + g_ref[0] * moe
    yn = _rms(xn, gn_ref[...])
    if final:
        ya_ref, yb_ref = rest[:2]

        @pl.when(i < n_split)
        def _():
            ya_ref[...] = yn

        @pl.when(i >= n_split)
        def _():
            yb_ref[...] = yn
    else:
        sc_ref, sh_ref, o_ref, u_ref = rest[:4]
        o_ref[...] = xn
        u_ref[...] = (yn * (1.0 + sc_ref[0]) + sh_ref[0]).astype(u_ref.dtype)


def moe_combine(x, y, pos, gate, mod3, g_chunk, row_fn, g_next, mod3_next=None, n_first=0, tm=128):
    M, D = x.shape
    n_tiles = M // tm
    final = mod3_next is None
    n_split = n_first // tm
    pos3 = pos.reshape(n_tiles, 1, TOP_K * tm)
    row_spec = pl.BlockSpec((tm, D), lambda i: (i, 0))
    in_specs = [
        pl.BlockSpec((1, 1, TOP_K * tm), lambda i: (i, 0, 0), memory_space=pltpu.SMEM),
        pl.BlockSpec((1, 1, TOP_K * tm), lambda i: (jnp.minimum(i + 1, n_tiles - 1), 0, 0),
                     memory_space=pltpu.SMEM),
        row_spec,
        pl.BlockSpec((1, 1, D), lambda i: (row_fn(i), 0, g_chunk)),
        pl.BlockSpec((tm, TOP_K), lambda i: (i, 0)),
        pl.BlockSpec(memory_space=pl.ANY),
        pl.BlockSpec((1, D), lambda i: (0, 0)),
    ]
    args = [pos3, pos3, x, mod3, gate, y, g_next.reshape(1, D)]
    scratch = [pltpu.VMEM((2, TOP_K, tm, D), F32), pltpu.SemaphoreType.DMA((2,))]
    kern = functools.partial(_combine_kernel, tm=tm, n_tiles=n_tiles, final=final, n_split=n_split)
    if final:
        out_specs = [pl.BlockSpec((tm, D), lambda i: (jnp.minimum(i, n_split - 1), 0)),
                     pl.BlockSpec((tm, D), lambda i: (jnp.maximum(i - n_split, 0), 0))]
        return pl.pallas_call(
            kern, grid=(n_tiles,), in_specs=in_specs, out_specs=out_specs,
            out_shape=[jax.ShapeDtypeStruct((n_first, D), F32),
                       jax.ShapeDtypeStruct((M - n_first, D), F32)],
            scratch_shapes=scratch, compiler_params=_cparams(1), name="moe_combine_final")(*args)
    in_specs += [pl.BlockSpec((1, 1, D), lambda i: (row_fn(i), 0, 1)),
                 pl.BlockSpec((1, 1, D), lambda i: (row_fn(i), 0, 0))]
    return pl.pallas_call(
        kern, grid=(n_tiles,), in_specs=in_specs, out_specs=[row_spec, row_spec],
        out_shape=[jax.ShapeDtypeStruct((M, D), F32), jax.ShapeDtypeStruct((M, D), BF16)],
        scratch_shapes=scratch, input_output_aliases={2: 0},
        compiler_params=_cparams(1), name="moe_combine")(*args, mod3_next, mod3_next)


def kernel(x_prompt, x_sample, cache_k, cache_v, state_rglru, c, c_ctx, w_ada, b_ada, norm_attn,
           norm_ffn, norm_final, w_in, w_pool, pool_scale, rpb, conv_w, conv_b, rg_w_a, rg_b_a,
           rg_w_i, rg_b_i, rg_lam, w_out, w_router, b_router, w_up, w_down):
    B, S, D = x_prompt.shape
    DB, T, _ = x_sample.shape
    L = w_in.shape[0]
    pool_w = w_pool.shape[1] * w_pool.shape[2]
    na_w = cache_k.shape[3] * cache_k.shape[4]
    rg_w = rg_lam.shape[-1]
    n_ctx = B * S
    n_tok = n_ctx + DB * T
    q_col0 = pool_w
    x_col0 = pool_w + 3 * na_w

    n_cond = 8
    cond = jnp.concatenate([c_ctx[None, :], c, jnp.zeros((n_cond - 1 - DB, D), F32)], axis=0)
    mod = ada_mod(cond, w_ada, b_ada)

    wr = jnp.zeros((D, 128), F32).at[:, :N_EXPERTS].set(w_router).astype(BF16)
    br = jnp.zeros((1, 128), F32).at[0, :N_EXPERTS].set(b_router)
    n_blocks = (n_tok * TOP_K) // MOE_BLOCK + N_EXPERTS

    ck4 = cache_k.reshape(DB, L, cache_k.shape[2], na_w)
    cv4 = cache_v.reshape(DB, L, cache_v.shape[2], na_w)
    rpb_flat = rpb.reshape(L, -1)
    rg_params = dict(conv_w=conv_w, conv_b=conv_b, rg_w_a=rg_w_a, rg_w_i=rg_w_i, rg_b_a=rg_b_a,
                     rg_b_i=rg_b_i, rg_lam=rg_lam)
    zeros_h = jnp.zeros((B, 2, rg_w), F32)

    new_k, new_v, new_h = [], [], []
    row256 = _cond_row_fn(256, n_ctx, T)
    row1024 = _cond_row_fn(1024, n_ctx, T)
    row128 = _cond_row_fn(128, n_ctx, T)
    mod3s = [mod[l].reshape(n_cond, 1, 6 * D) for l in range(L)]
    x, u = norm_mod_first(x_prompt.reshape(n_ctx, D), x_sample.reshape(DB * T, D), norm_attn[0],
                          mod3s[0], 1, 0, row256)
    for l in range(L):
        mod3 = mod3s[l]
        proj = mm_ws(u, w_in, l, tm=1024, tn=1024)

        kv = proj[:n_ctx, q_col0 + na_w:q_col0 + 3 * na_w]
        new_k.append(kv[:, :na_w].reshape(B, S, -1, NA_HEAD_DIM))
        new_v.append(kv[:, na_w:].reshape(B, S, -1, NA_HEAD_DIM))

        mix = pool_mixer(proj, w_pool, pool_scale, l, None, S, B, 0, D)
        mix = pool_mixer(proj, w_pool, pool_scale, l, mix, T, DB, n_ctx // T, D)
        mix = ctx_attention(proj, mix, S, B, na_w, q_col0)
        mix = natten(proj, ck4, cv4, rpb_flat, l, mix, T, DB, n_ctx // T, q_col0, na_w)
        mix, h_ctx = rglru(proj, rg_params, l, zeros_h, mix, S, B, 0, x_col0, pool_w + na_w)
        mix, _ = rglru(proj, rg_params, l, state_rglru[:, l], mix, T, DB, n_ctx // T, x_col0,
                       pool_w + na_w)
        new_h.append(h_ctx)

        x = mm_ws(mix, w_out, l, tm=1024, tn=512, res=x, gate=(mod3, 2, row1024))

        u2, logits = norm_mod_router(x, norm_ffn[l], mod3, 4, 3, row256, F32, (wr, br))
        slot_tok, gate, block_e, n_active, pos = route(logits[:, :N_EXPERTS], n_blocks)
        xs = gather_rows(u2, slot_tok, n_active, n_blocks)
        hmid = moe_up(xs, w_up, l, block_e, n_active)
        y = moe_down(hmid, w_down, l, block_e, n_active)
        if l + 1 < L:
            x, u = moe_combine(x, y, pos, gate, mod3, 5, row128, norm_attn[l + 1], mod3s[l + 1])
        else:
            y_ctx, y_lat = moe_combine(x, y, pos, gate, mod3, 5, row128, norm_final, n_first=n_ctx)

    return (y_ctx.reshape(B, S, D), y_lat.reshape(DB, T, D), jnp.stack(new_k, axis=1), jnp.stack(new_v, axis=1),
            jnp.stack(new_h, axis=1))
```

```python
import functools

import jax
import jax.numpy as jnp
from jax import lax
from jax.experimental import pallas as pl
from jax.experimental.pallas import tpu as pltpu

F32 = jnp.float32
BF16 = jnp.bfloat16

EPS = 1e-6
GRID_W = 64
POOL_WINDOWS = (2, 4, 8, 16)
N_POOL_GROUPS = 4
WIN_R = 8
WIN_C = 16
NA_HEAD_DIM = 128
RG_BLOCK_DIM = 128
CONV_W = 4
RG_C = 8.0
N_EXPERTS = 16
N_EXPERT_GROUPS = 4
GROUP_SIZE = N_EXPERTS // N_EXPERT_GROUPS
TOP_K = 2
MOE_BLOCK = 512

NEG_BIG = -1e30
LOG2E = 1.4426950408889634
VMEM_LIMIT_BYTES = 60 * 1024 * 1024

NAT_R = 4
NAT_NK = 12


def _cparams(n_axes):
    return pltpu.CompilerParams(
        dimension_semantics=("arbitrary",) * n_axes, vmem_limit_bytes=VMEM_LIMIT_BYTES)


def _silu(x):
    return x * jax.nn.sigmoid(x)


def _sigmoid_tanh(x):
    return 0.5 * jnp.tanh(0.5 * x) + 0.5


def _ada_kernel(c_ref, w_ref, b_ref, o_ref):
    s = _silu(c_ref[...]).astype(BF16)
    w = w_ref[...].astype(BF16)
    o_ref[...] = jnp.dot(s, w, preferred_element_type=F32) + b_ref[...]


def ada_mod(cond, w_ada, b_ada, tn=512):
    L, D, N = w_ada.shape
    R = cond.shape[0]
    return pl.pallas_call(
        _ada_kernel,
        grid=(L, N // tn),
        in_specs=[
            pl.BlockSpec((R, D), lambda l, j: (0, 0)),
            pl.BlockSpec((None, D, tn), lambda l, j: (l, 0, j)),
            pl.BlockSpec((None, 1, tn), lambda l, j: (l, 0, j)),
        ],
        out_specs=pl.BlockSpec((None, R, tn), lambda l, j: (l, 0, j)),
        out_shape=jax.ShapeDtypeStruct((L, R, N), F32),
        compiler_params=_cparams(2),
        name="ada_mod",
    )(cond, w_ada, b_ada.reshape(L, 1, N))


def _rms(x, g):
    ms = jnp.mean(x * x, axis=-1, keepdims=True)
    return x * lax.rsqrt(ms + EPS) * g


HI16 = 0xFFFF0000


def _pack_bf16_pairs(ub):
    half = ub.shape[1] // 2
    bits = lax.bitcast_convert_type(ub.astype(F32), jnp.uint32)
    return (bits[:, half:] & jnp.uint32(HI16)) | (bits[:, :half] >> 16)


def _unpack_bf16_pairs(w):
    lo = lax.bitcast_convert_type(w << 16, F32).astype(BF16)
    hi = lax.bitcast_convert_type(w & jnp.uint32(HI16), F32).astype(BF16)
    return lo, hi


def _norm_mod_router_kernel(x_ref, g_ref, sc_ref, sh_ref, wr_ref, br_ref, u_ref, lg_ref):
    y = _rms(x_ref[...], g_ref[...])
    ub = (y * (1.0 + sc_ref[0]) + sh_ref[0]).astype(BF16)
    u_ref[...] = _pack_bf16_pairs(ub)
    lg_ref[...] = jnp.dot(ub, wr_ref[...], preferred_element_type=F32) + br_ref[...]


def _norm_mod_first_kernel(xa_ref, xb_ref, g_ref, sc_ref, sh_ref, x_ref, u_ref, *, n_a):
    x = jnp.where(pl.program_id(0) < n_a, xa_ref[...], xb_ref[...])
    x_ref[...] = x
    u_ref[...] = (_rms(x, g_ref[...]) * (1.0 + sc_ref[0]) + sh_ref[0]).astype(u_ref.dtype)


def norm_mod_first(xa, xb, g, mod3, sc_chunk, sh_chunk, row_fn, tm=256):
    D = xa.shape[1]
    n_a, n_b = xa.shape[0] // tm, xb.shape[0] // tm
    M = xa.shape[0] + xb.shape[0]
    row_spec = pl.BlockSpec((tm, D), lambda i: (i, 0))
    return pl.pallas_call(
        functools.partial(_norm_mod_first_kernel, n_a=n_a), grid=(n_a + n_b,),
        in_specs=[
            pl.BlockSpec((tm, D), lambda i: (jnp.minimum(i, n_a - 1), 0)),
            pl.BlockSpec((tm, D), lambda i: (jnp.maximum(i - n_a, 0), 0)),
            pl.BlockSpec((1, D), lambda i: (0, 0)),
            pl.BlockSpec((1, 1, D), lambda i: (row_fn(i), 0, sc_chunk)),
            pl.BlockSpec((1, 1, D), lambda i: (row_fn(i), 0, sh_chunk)),
        ],
        out_specs=[row_spec, row_spec],
        out_shape=[jax.ShapeDtypeStruct((M, D), F32), jax.ShapeDtypeStruct((M, D), BF16)],
        compiler_params=_cparams(1), name="norm_mod_first")(xa, xb, g.reshape(1, D), mod3, mod3)


def _cond_row_fn(tm, n_ctx_rows, dec_seq):
    def row(i):
        r0 = i * tm
        return jnp.where(r0 < n_ctx_rows, 0, 1 + (r0 - n_ctx_rows) // dec_seq)
    return row


def norm_mod_router(x, g, mod3, sc_chunk, sh_chunk, row_fn, router, tm=256):
    M, D = x.shape
    wr, br = router
    NE = wr.shape[1]
    in_specs = [
        pl.BlockSpec((tm, D), lambda i: (i, 0)),
        pl.BlockSpec((1, D), lambda i: (0, 0)),
        pl.BlockSpec((1, 1, D), lambda i: (row_fn(i), 0, sc_chunk)),
        pl.BlockSpec((1, 1, D), lambda i: (row_fn(i), 0, sh_chunk)),
        pl.BlockSpec((D, NE), lambda i: (0, 0)),
        pl.BlockSpec((1, NE), lambda i: (0, 0)),
    ]
    return pl.pallas_call(
        _norm_mod_router_kernel, grid=(M // tm,), in_specs=in_specs,
        out_specs=[pl.BlockSpec((tm, D // 2), lambda i: (i, 0)),
                   pl.BlockSpec((tm, NE), lambda i: (i, 0))],
        out_shape=[jax.ShapeDtypeStruct((M, D // 2), jnp.uint32),
                   jax.ShapeDtypeStruct((M, NE), F32)],
        compiler_params=_cparams(1), name="norm_mod_router")(x, g.reshape(1, D), mod3, mod3, wr, br)


def _cast_weight(w_ref, wbf_ref, kchunk=512):
    K = w_ref.shape[0]
    for k0 in range(0, K, kchunk):
        k1 = min(K, k0 + kchunk)
        wbf_ref[k0:k1, :] = w_ref[k0:k1, :].astype(BF16)


def _mm_kernel(lhs_ref, w_ref, o_ref, wbf_ref):
    @pl.when(pl.program_id(1) == 0)
    def _():
        _cast_weight(w_ref, wbf_ref)
    o_ref[...] = jnp.dot(lhs_ref[...], wbf_ref[...], preferred_element_type=F32).astype(o_ref.dtype)


def _mm_res_kernel(lhs_ref, w_ref, res_ref, g_ref, o_ref, wbf_ref):
    @pl.when(pl.program_id(1) == 0)
    def _():
        _cast_weight(w_ref, wbf_ref)
    acc = jnp.dot(lhs_ref[...], wbf_ref[...], preferred_element_type=F32)
    o_ref[...] = res_ref[...] + g_ref[0] * acc


def mm_ws(lhs, w_all, layer, tm, tn, res=None, gate=None, out_dtype=F32):
    M, K = lhs.shape
    N = w_all.shape[-1]
    in_specs = [
        pl.BlockSpec((tm, K), lambda j, i: (i, 0)),
        pl.BlockSpec((None, K, tn), lambda j, i: (layer, 0, j), pipeline_mode=pl.Buffered(1)),
    ]
    out_spec = pl.BlockSpec((tm, tn), lambda j, i: (i, j))
    scratch = [pltpu.VMEM((K, tn), BF16)]
    if res is None:
        return pl.pallas_call(
            _mm_kernel, grid=(N // tn, M // tm), in_specs=in_specs, out_specs=out_spec,
            out_shape=jax.ShapeDtypeStruct((M, N), out_dtype), scratch_shapes=scratch,
            compiler_params=_cparams(2), name="mm_ws")(lhs, w_all)
    mod3, g_chunk, row_fn = gate
    in_specs += [
        pl.BlockSpec((tm, tn), lambda j, i: (i, j)),
        pl.BlockSpec((1, 1, tn), lambda j, i: (row_fn(i), 0, g_chunk * (N // tn) + j)),
    ]
    return pl.pallas_call(
        _mm_res_kernel, grid=(N // tn, M // tm), in_specs=in_specs, out_specs=out_spec,
        out_shape=jax.ShapeDtypeStruct((M, N), F32), scratch_shapes=scratch,
        input_output_aliases={2: 0},
        compiler_params=_cparams(2), name="mm_ws_res")(lhs, w_all, res, mod3)


POOL_PAD = 16


def _pool_kernel(x_ref, w_ref, s_ref, *rest, T, CH):
    o_ref, xpad = rest[-2], rest[-1]
    g = pl.program_id(1)
    C = x_ref.shape[1]
    xpad[0:POOL_PAD, :] = jnp.zeros((POOL_PAD, C), F32)
    xpad[T + POOL_PAD:T + 2 * POOL_PAD, :] = jnp.zeros((POOL_PAD, C), F32)
    for c in range(T // CH):
        xpad[POOL_PAD + c * CH:POOL_PAD + (c + 1) * CH, :] = x_ref[c * CH:(c + 1) * CH, :]
    wbf = w_ref[...].astype(BF16)
    scale = s_ref[...]
    for gi, win in enumerate(POOL_WINDOWS):
        half = win // 2

        @pl.when(g == gi)
        def _():
            for c in range(T // CH):
                base = c * CH
                acc = xpad[base + POOL_PAD - half:base + POOL_PAD - half + CH, :]
                for d in range(-half + 1, half):
                    acc = acc + xpad[base + POOL_PAD + d:base + POOL_PAD + d + CH, :]
                t = base + lax.broadcasted_iota(jnp.int32, (CH, 1), 0)
                cnt = (jnp.minimum(t + half, T) - jnp.maximum(t - half, 0)).astype(F32)
                xc = xpad[base + POOL_PAD:base + POOL_PAD + CH, :]
                pooled = (acc / cnt - xc).astype(BF16)
                y = jnp.dot(pooled, wbf, preferred_element_type=F32) * scale
                o_ref[base:base + CH, :] = y.astype(o_ref.dtype)


def pool_mixer(proj, w_pool_all, pool_scale_all, layer, mix, T, n_seq, row_blk0, mix_cols):
    C = w_pool_all.shape[-1]
    W = N_POOL_GROUPS * C
    in_specs = [
        pl.BlockSpec((T, C), lambda b, g: (row_blk0 + b, g)),
        pl.BlockSpec((None, None, C, C), lambda b, g: (layer, g, 0, 0)),
        pl.BlockSpec((None, 1, C), lambda b, g: (layer, 0, g)),
    ]
    args = [proj, w_pool_all, pool_scale_all.reshape(-1, 1, W)]
    aliases = {}
    if mix is not None:
        in_specs.append(pl.BlockSpec(memory_space=pl.ANY))
        args.append(mix)
        aliases = {3: 0}
    return pl.pallas_call(
        functools.partial(_pool_kernel, T=T, CH=min(T, 256)),
        grid=(n_seq, N_POOL_GROUPS), in_specs=in_specs,
        out_specs=pl.BlockSpec((T, C), lambda b, g: (row_blk0 + b, g)),
        out_shape=jax.ShapeDtypeStruct((proj.shape[0], mix_cols), BF16),
        scratch_shapes=[pltpu.VMEM((T + 2 * POOL_PAD, C), F32)],
        input_output_aliases=aliases,
        compiler_params=_cparams(2), name=f"pool_T{T}")(*args)


def _ctx_attn_kernel(q_ref, k_ref, v_ref, mix_ref, o_ref, *, n_heads):
    del mix_ref
    Dh = NA_HEAD_DIM
    scale = Dh ** -0.5
    for h in range(n_heads):
        sl = slice(h * Dh, (h + 1) * Dh)
        q = q_ref[:, sl].astype(BF16)
        k = k_ref[:, sl].astype(BF16)
        v = v_ref[:, sl].astype(BF16)
        s = lax.dot_general(q, k, (((1,), (1,)), ((), ())), preferred_element_type=F32) * scale
        m = jnp.max(s, axis=-1, keepdims=True)
        p = jnp.exp(s - m)
        l = jnp.sum(p, axis=-1, keepdims=True)
        o = jnp.dot(p.astype(BF16), v, preferred_element_type=F32)
        o_ref[:, sl] = (o * (1.0 / l)).astype(o_ref.dtype)


def ctx_attention(proj, mix, S, n_seq, na_width, q_col0):
    cb = 1024
    nhalf = na_width // cb
    qb, kb, vb = q_col0 // cb, (q_col0 + na_width) // cb, (q_col0 + 2 * na_width) // cb
    return pl.pallas_call(
        functools.partial(_ctx_attn_kernel, n_heads=cb // NA_HEAD_DIM),
        grid=(n_seq, nhalf),
        in_specs=[
            pl.BlockSpec((S, cb), lambda b, c: (b, qb + c)),
            pl.BlockSpec((S, cb), lambda b, c: (b, kb + c)),
            pl.BlockSpec((S, cb), lambda b, c: (b, vb + c)),
            pl.BlockSpec(memory_space=pl.ANY),
        ],
        out_specs=pl.BlockSpec((S, cb), lambda b, c: (b, qb + c)),
        out_shape=jax.ShapeDtypeStruct(mix.shape, mix.dtype),
        input_output_aliases={3: 0},
        compiler_params=_cparams(2), name="ctx_attn")(proj, proj, proj, mix)


def _natten_cases(rows):
    R, NK = NAT_R, NAT_NK
    cases = []
    for r0 in (0, R, rows - R):
        start = min(max(r0 - WIN_R // 2, 0), rows - NK)
        tiles = []
        for a in range(R):
            r = r0 + a
            rs = min(max(r - WIN_R // 2, 0), rows - WIN_R)
            row = []
            for i in range(NK):
                kr = start + i
                row.append(kr - r + (WIN_R - 1) if rs <= kr < rs + WIN_R else None)
            tiles.append(row)
        cases.append((start - r0, tiles))
    return cases


def _natten_kernel(rpb_ref, q_ref, k_ref, v_ref, ck_ref, cv_ref, mix_ref, o_ref,
                   qbf, kbf, vbf, ckbf, cvbf, tb, bm, *, rows):
    del mix_ref
    W = GRID_W
    R, NK = NAT_R, NAT_NK
    RQ, NKK = R * W, NK * W
    Dh = NA_HEAD_DIM
    scale2 = Dh ** -0.5 * LOG2E
    h = pl.program_id(1)
    n_dr, n_dc = 2 * WIN_R - 1, 2 * WIN_C - 1
    T = rows * W

    for c0 in range(0, T, 512):
        qbf[c0:c0 + 512, :] = q_ref[c0:c0 + 512, :].astype(BF16)
        kbf[c0:c0 + 512, :] = k_ref[c0:c0 + 512, :].astype(BF16)
        vbf[c0:c0 + 512, :] = v_ref[c0:c0 + 512, :].astype(BF16)
    ckbf[...] = ck_ref[...].astype(BF16)
    cvbf[...] = cv_ref[...].astype(BF16)

    ci = lax.broadcasted_iota(jnp.int32, (W, W), 0)
    kci = lax.broadcasted_iota(jnp.int32, (W, W), 1)
    dci = kci - ci + (WIN_C - 1)
    cs = jnp.clip(ci - WIN_C // 2, 0, W - WIN_C)
    in_win = (kci >= cs) & (kci < cs + WIN_C)
    for dr in range(n_dr):
        t = jnp.full((W, W), NEG_BIG, F32)
        for dc in range(n_dc):
            t = jnp.where(dci == dc, rpb_ref[h * (n_dr * n_dc) + dr * n_dc + dc] * LOG2E, t)
        tb[dr] = jnp.where(in_win, t, NEG_BIG)

    cases = _natten_cases(rows)
    neg_tile = jnp.full((W, W), NEG_BIG, F32)
    for ci_, (_, tiles) in enumerate(cases):
        for a in range(R):
            for i in range(0, NK, 2):
                pair = [tb[d] if d is not None else neg_tile for d in tiles[a][i:i + 2]]
                bm[ci_, a * W:(a + 1) * W, i * W:(i + 2) * W] = jnp.concatenate(pair, axis=1)

    nt = (((1,), (1,)), ((), ()))

    def block(q0, k0, case):
        qb = qbf[pl.ds(q0, RQ), :]
        kk = kbf[pl.ds(k0, NKK), :]
        vv = vbf[pl.ds(k0, NKK), :]
        s1 = lax.dot_general(qb, kk, nt, preferred_element_type=F32) * scale2 + bm[case]
        s2 = lax.dot_general(qb, ckbf[...], nt, preferred_element_type=F32) * scale2
        m = jnp.maximum(jnp.max(s1, axis=-1, keepdims=True), jnp.max(s2, axis=-1, keepdims=True))
        p1 = jnp.exp2(s1 - m)
        p2 = jnp.exp2(s2 - m)
        l = jnp.sum(p1, axis=-1, keepdims=True) + jnp.sum(p2, axis=-1, keepdims=True)
        o = (jnp.dot(p1.astype(BF16), vv, preferred_element_type=F32)
             + jnp.dot(p2.astype(BF16), cvbf[...], preferred_element_type=F32))
        o_ref[pl.ds(q0, RQ), :] = (o * (1.0 / l)).astype(o_ref.dtype)

    block(0, (cases[0][0]) * W, 0)

    for blk in range(1, rows // R - 1):
        block(blk * RQ, (blk * R + cases[1][0]) * W, 1)
    q_last = (rows - R) * W
    block(q_last, q_last + cases[2][0] * W, 2)


def natten(proj, cache_k, cache_v, rpb_all, layer, mix, T, n_seq, row_blk0, q_col0, na_width):
    Dh = NA_HEAD_DIM
    H = na_width // Dh
    P = cache_k.shape[2]
    rows = T // GRID_W
    qb, kb, vb = q_col0 // Dh, (q_col0 + na_width) // Dh, (q_col0 + 2 * na_width) // Dh
    return pl.pallas_call(
        functools.partial(_natten_kernel, rows=rows),
        grid=(n_seq, H),
        in_specs=[
            pl.BlockSpec(memory_space=pltpu.SMEM),
            pl.BlockSpec((T, Dh), lambda b, h: (row_blk0 + b, qb + h)),
            pl.BlockSpec((T, Dh), lambda b, h: (row_blk0 + b, kb + h)),
            pl.BlockSpec((T, Dh), lambda b, h: (row_blk0 + b, vb + h)),
            pl.BlockSpec((None, None, P, Dh), lambda b, h: (b, layer, 0, h)),
            pl.BlockSpec((None, None, P, Dh), lambda b, h: (b, layer, 0, h)),
            pl.BlockSpec(memory_space=pl.ANY),
        ],
        out_specs=pl.BlockSpec((T, Dh), lambda b, h: (row_blk0 + b, qb + h)),
        out_shape=jax.ShapeDtypeStruct(mix.shape, mix.dtype),
        scratch_shapes=[
            pltpu.VMEM((T, Dh), BF16), pltpu.VMEM((T, Dh), BF16), pltpu.VMEM((T, Dh), BF16),
            pltpu.VMEM((P, Dh), BF16), pltpu.VMEM((P, Dh), BF16),
            pltpu.VMEM((2 * WIN_R - 1, GRID_W, GRID_W), F32),
            pltpu.VMEM((3, NAT_R * GRID_W, NAT_NK * GRID_W), F32),
        ],
        input_output_aliases={6: 0},
        compiler_params=_cparams(2), name="natten")(
            rpb_all[layer], proj, proj, proj, cache_k, cache_v, mix)


RG_PAD = 8
SCAN_ROWS = 8


def _scan8(a, b, reverse):
    row = lax.broadcasted_iota(jnp.int32, a.shape, 0)
    for s in (1, 2, 4):
        if reverse:
            sh, keep = SCAN_ROWS - s, row < SCAN_ROWS - s
        else:
            sh, keep = s, row >= s
        a_s = pltpu.roll(a, sh, 0)
        b_s = pltpu.roll(b, sh, 0)
        b = jnp.where(keep, a * b_s + b, b)
        a = jnp.where(keep, a * a_s, a)
    return a, b


def _gelu_tanh(x):
    return 0.5 * x * (1.0 + jnp.tanh(0.7978845608028654 * (x + 0.044715 * (x * x * x))))


def _rglru_kernel(x_ref, gt_ref, cw_ref, cb_ref, wa_ref, wi_ref, ba_ref, bi_ref, lam_ref, h0_ref,
                  *rest, T, CH, U):
    y_ref, hs_ref, xpad, a_f, b_f, a_b, b_b, hf_s, hb_s = rest[-9:]
    C = x_ref.shape[1]
    xpad[0:RG_PAD, :] = jnp.zeros((RG_PAD, C), F32)
    xpad[T + RG_PAD:T + 2 * RG_PAD, :] = jnp.zeros((RG_PAD, C), F32)
    for c in range(T // CH):
        xpad[RG_PAD + c * CH:RG_PAD + (c + 1) * CH, :] = x_ref[c * CH:(c + 1) * CH, :]

    wcat = jnp.concatenate([wa_ref[0], wa_ref[1], wi_ref[0], wi_ref[1]], axis=1).astype(BF16)
    bcat = jnp.concatenate([ba_ref[0], ba_ref[1], bi_ref[0], bi_ref[1]], axis=1)
    z = -lam_ref[...]
    sp = jnp.maximum(z, 0.0) + jnp.log1p(jnp.exp(-jnp.abs(z)))
    cw = cw_ref[...]
    cb = cb_ref[...]
    lead = RG_PAD - CONV_W // 2

    def gates(c, carry):
        c0 = pl.multiple_of(c * CH, CH)
        win = xpad[pl.ds(c0, CH + 2 * RG_PAD), :]
        xc = cb
        for k in range(CONV_W):
            xc = xc + win[lead + k:lead + k + CH, :] * cw[k:k + 1, :]
        gz = jnp.dot(xc.astype(BF16), wcat, preferred_element_type=F32) + bcat
        for d, (a_s, b_s) in enumerate(((a_f, b_f), (a_b, b_b))):
            r = _sigmoid_tanh(gz[:, d * C:(d + 1) * C])
            ig = _sigmoid_tanh(gz[:, (2 + d) * C:(3 + d) * C])
            log_a = -RG_C * r * sp[d]
            a = jnp.exp(log_a)
            a_s[pl.ds(c0, CH), :] = a
            b_s[pl.ds(c0, CH), :] = jnp.sqrt(1.0 - a * a) * ig * xc
        return carry

    lax.fori_loop(0, T // CH, gates, 0)

    RU = SCAN_ROWS * U
    n_it = T // RU

    def scan(j, carry):
        hf, hb = carry
        r0 = pl.multiple_of(j * RU, RU)
        r1 = pl.multiple_of((n_it - 1 - j) * RU, RU)
        af, bf = a_f[pl.ds(r0, RU), :], b_f[pl.ds(r0, RU), :]
        ab, bb = a_b[pl.ds(r1, RU), :], b_b[pl.ds(r1, RU), :]
        outs_f, outs_b = [None] * U, [None] * U
        for u in range(U):
            a8, b8 = _scan8(af[u * 8:(u + 1) * 8], bf[u * 8:(u + 1) * 8], False)
            rows_f = b8 + a8 * hf
            outs_f[u] = rows_f
            hf = jnp.broadcast_to(rows_f[SCAN_ROWS - 1:SCAN_ROWS, :], rows_f.shape)
            v = U - 1 - u
            a8, b8 = _scan8(ab[v * 8:(v + 1) * 8], bb[v * 8:(v + 1) * 8], True)
            rows_b = b8 + a8 * hb
            outs_b[v] = rows_b
            hb = jnp.broadcast_to(rows_b[0:1, :], rows_b.shape)
        hf_s[pl.ds(r0, RU), :] = jnp.concatenate(outs_f, axis=0)
        hb_s[pl.ds(r1, RU), :] = jnp.concatenate(outs_b, axis=0)
        return hf, hb

    h0 = h0_ref[...]
    hf_last, hb_first = lax.fori_loop(
        0, n_it, scan, (jnp.broadcast_to(h0[0:1, :], (SCAN_ROWS, C)),
                        jnp.broadcast_to(h0[1:2, :], (SCAN_ROWS, C))))
    hs_ref[0:1, :] = hf_last[SCAN_ROWS - 1:SCAN_ROWS, :]
    hs_ref[1:2, :] = hb_first[0:1, :]

    def emit(c, carry):
        c0 = pl.multiple_of(c * CH, CH)
        h = hf_s[pl.ds(c0, CH), :] + hb_s[pl.ds(c0, CH), :]
        y_ref[pl.ds(c0, CH), :] = (h * _gelu_tanh(gt_ref[pl.ds(c0, CH), :])).astype(y_ref.dtype)
        return carry

    lax.fori_loop(0, T // CH, emit, 0)


def rglru(proj, p, layer, h0, mix, T, n_seq, row_blk0, x_col0, mix_col0):
    C = RG_BLOCK_DIM
    W = h0.shape[-1]
    nb = W // C
    xb, gb, mb = x_col0 // C, (x_col0 + W) // C, mix_col0 // C
    L = p['conv_w'].shape[0]
    vec = lambda arr: arr.reshape(L, -1, 1, W)
    in_specs = [
        pl.BlockSpec((T, C), lambda b, n: (row_blk0 + b, xb + n)),
        pl.BlockSpec((T, C), lambda b, n: (row_blk0 + b, gb + n)),
        pl.BlockSpec((None, CONV_W, C), lambda b, n: (layer, 0, n)),
        pl.BlockSpec((None, 1, C), lambda b, n: (layer, 0, n)),
        pl.BlockSpec((None, 2, None, C, C), lambda b, n: (layer, 0, n, 0, 0)),
        pl.BlockSpec((None, 2, None, C, C), lambda b, n: (layer, 0, n, 0, 0)),
        pl.BlockSpec((None, 2, 1, C), lambda b, n: (layer, 0, 0, n)),
        pl.BlockSpec((None, 2, 1, C), lambda b, n: (layer, 0, 0, n)),
        pl.BlockSpec((None, 2, 1, C), lambda b, n: (layer, 0, 0, n)),
        pl.BlockSpec((None, 2, C), lambda b, n: (b, 0, n)),
        pl.BlockSpec(memory_space=pl.ANY),
    ]
    args = [proj, proj, p['conv_w'], p['conv_b'].reshape(L, 1, W), p['rg_w_a'], p['rg_w_i'],
            vec(p['rg_b_a']), vec(p['rg_b_i']), vec(p['rg_lam']), h0, mix]
    return pl.pallas_call(
        functools.partial(_rglru_kernel, T=T, CH=min(T, 256), U=4),
        grid=(n_seq, nb), in_specs=in_specs,
        out_specs=[
            pl.BlockSpec((T, C), lambda b, n: (row_blk0 + b, mb + n)),
            pl.BlockSpec((None, 2, C), lambda b, n: (b, 0, n)),
        ],
        out_shape=[jax.ShapeDtypeStruct(mix.shape, mix.dtype),
                   jax.ShapeDtypeStruct((n_seq, 2, W), F32)],
        scratch_shapes=[pltpu.VMEM((T + 2 * RG_PAD, C), F32)] + [pltpu.VMEM((T, C), F32)] * 6,
        input_output_aliases={10: 0},
        compiler_params=_cparams(2), name=f"rglru_T{T}")(*args)


def route(logits, n_blocks):
    N = logits.shape[0]
    lg = logits.reshape(N, N_EXPERT_GROUPS, GROUP_SIZE)
    lane = jnp.arange(GROUP_SIZE, dtype=jnp.int32)
    i1 = jnp.argmax(lg, axis=-1).astype(jnp.int32)
    v1 = jnp.max(lg, axis=-1)
    rest = jnp.where(lane == i1[..., None], -jnp.inf, lg)
    i2 = jnp.argmax(rest, axis=-1).astype(jnp.int32)
    v2 = jnp.max(rest, axis=-1)
    g_sel = jnp.argmax(v1 + v2, axis=-1).astype(jnp.int32)
    pick = lambda a: jnp.take_along_axis(a, g_sel[:, None], axis=1)[:, 0]
    sel_v = jnp.stack([pick(v1), pick(v2)], axis=-1)
    sel_i = jnp.stack([pick(i1), pick(i2)], axis=-1)
    expert = g_sel[:, None] * GROUP_SIZE + sel_i
    gate = jax.nn.softmax(sel_v, axis=-1)

    A = N * TOP_K
    flat_e = expert.reshape(A)
    onehot = (flat_e[:, None] == jnp.arange(N_EXPERTS, dtype=jnp.int32)[None, :]).astype(F32)
    ch = 256
    oh3 = onehot.reshape(A // ch, ch, N_EXPERTS)
    tri = (jnp.arange(ch)[:, None] > jnp.arange(ch)[None, :]).astype(F32)
    local = jnp.einsum('ij,bjk->bik', tri, oh3, precision=lax.Precision.HIGHEST)
    chunk_sum = oh3.sum(axis=1)
    chunk_off = jnp.cumsum(chunk_sum, axis=0) - chunk_sum
    rank = jnp.sum((local + chunk_off[:, None, :]) * oh3, axis=-1).reshape(A).astype(jnp.int32)
    counts = chunk_sum.sum(axis=0).astype(jnp.int32)
    padded = ((counts + MOE_BLOCK - 1) // MOE_BLOCK) * MOE_BLOCK
    pad_end = jnp.cumsum(padded)
    pad_start = pad_end - padded
    dest = (jnp.sum(onehot * pad_start.astype(F32)[None, :], axis=-1).astype(jnp.int32) + rank)
    P = n_blocks * MOE_BLOCK
    flat_tok = jnp.arange(A, dtype=jnp.int32) // TOP_K
    slot_tok = jnp.zeros((P,), jnp.int32).at[dest].set(flat_tok)
    blk_start = jnp.arange(n_blocks, dtype=jnp.int32) * MOE_BLOCK
    block_e = jnp.minimum(jnp.sum((pad_end[None, :] <= blk_start[:, None]).astype(jnp.int32), axis=1),
                          N_EXPERTS - 1).astype(jnp.int32)
    n_active = (pad_end[-1] // MOE_BLOCK).astype(jnp.int32).reshape(1)
    return slot_tok, gate, block_e, n_active, dest.reshape(N, TOP_K)


def _row_copy(src, dst, s, d, sem):
    return pltpu.make_async_copy(src.at[pl.ds(s, 1)], dst.at[pl.ds(d, 1)], sem)


def _issue_rows(idx_ref, n_rows, src_ref, dst_ref, sem, unroll=8):
    def issue(c, carry):
        for u in range(unroll):
            r = c * unroll + u
            _row_copy(src_ref, dst_ref, idx_ref[0, 0, r], r, sem).start()
        return carry
    lax.fori_loop(0, n_rows // unroll, issue, 0)


def _wait_rows(n_rows, src_ref, dst_ref, sem):
    pltpu.make_async_copy(src_ref.at[pl.ds(0, n_rows)], dst_ref, sem).wait()


def _gather_kernel(nact_ref, idx_ref, idx_next_ref, src_ref, dst_ref, buf, sems):
    i = pl.program_id(0)
    nact = nact_ref[0]
    slot = i % 2

    @pl.when(i == 0)
    def _():
        _issue_rows(idx_ref, MOE_BLOCK, src_ref, buf.at[0], sems.at[0])

    @pl.when(i + 1 < nact)
    def _():
        _issue_rows(idx_next_ref, MOE_BLOCK, src_ref, buf.at[1 - slot], sems.at[1 - slot])

    @pl.when(i < nact)
    def _():
        _wait_rows(MOE_BLOCK, src_ref, buf.at[slot], sems.at[slot])
        half = src_ref.shape[1]
        lo, hi = _unpack_bf16_pairs(buf[slot])
        dst_ref[:, :half] = lo
        dst_ref[:, half:] = hi

    @pl.when(i >= nact)
    def _():
        dst_ref[...] = jnp.zeros(dst_ref.shape, dst_ref.dtype)


def gather_rows(src, slot_tok, n_active, n_blocks):
    Dp = src.shape[1]
    D = 2 * Dp
    last = n_blocks - 1
    grid_spec = pltpu.PrefetchScalarGridSpec(
        num_scalar_prefetch=1, grid=(n_blocks,),
        in_specs=[
            pl.BlockSpec((1, 1, MOE_BLOCK), lambda i, na: (i, 0, 0), memory_space=pltpu.SMEM),
            pl.BlockSpec((1, 1, MOE_BLOCK), lambda i, na: (jnp.minimum(i + 1, last), 0, 0),
                         memory_space=pltpu.SMEM),
            pl.BlockSpec(memory_space=pl.ANY),
        ],
        out_specs=pl.BlockSpec((MOE_BLOCK, D), lambda i, na: (i, 0)),
        scratch_shapes=[pltpu.VMEM((2, MOE_BLOCK, Dp), src.dtype), pltpu.SemaphoreType.DMA((2,))],
    )
    idx3 = slot_tok.reshape(n_blocks, 1, MOE_BLOCK)
    return pl.pallas_call(
        _gather_kernel, grid_spec=grid_spec,
        out_shape=jax.ShapeDtypeStruct((n_blocks * MOE_BLOCK, D), BF16),
        compiler_params=_cparams(1), name="moe_gather")(n_active, idx3, idx3, src)


def _new_expert(be_ref, i):
    return jnp.logical_or(i == 0, be_ref[i] != be_ref[jnp.maximum(i - 1, 0)])


def _moe_up_kernel(be_ref, nact_ref, xs_ref, wg_ref, wu_ref, h_ref, wg_bf, wu_bf):
    i = pl.program_id(1)

    @pl.when(_new_expert(be_ref, i))
    def _():
        _cast_weight(wg_ref, wg_bf)
        _cast_weight(wu_ref, wu_bf)

    @pl.when(i < nact_ref[0])
    def _():
        x = xs_ref[...]
        hg = jnp.dot(x, wg_bf[...], preferred_element_type=F32)
        hu = jnp.dot(x, wu_bf[...], preferred_element_type=F32)
        h_ref[...] = (_silu(hg) * hu).astype(h_ref.dtype)

    @pl.when(i >= nact_ref[0])
    def _():
        h_ref[...] = jnp.zeros(h_ref.shape, h_ref.dtype)


def moe_up(xs, w_up_all, layer, block_e, n_active, fc=512):
    P, D = xs.shape
    F = w_up_all.shape[-1] // 2
    n_blocks = P // MOE_BLOCK
    nfc = F // fc
    grid_spec = pltpu.PrefetchScalarGridSpec(
        num_scalar_prefetch=2, grid=(nfc, n_blocks),
        in_specs=[
            pl.BlockSpec((MOE_BLOCK, D), lambda j, i, be, na: (i, 0)),
            pl.BlockSpec((None, None, D, fc), lambda j, i, be, na: (layer, be[i], 0, j)),
            pl.BlockSpec((None, None, D, fc), lambda j, i, be, na: (layer, be[i], 0, nfc + j)),
        ],
        out_specs=pl.BlockSpec((MOE_BLOCK, fc), lambda j, i, be, na: (i, j)),
        scratch_shapes=[pltpu.VMEM((D, fc), BF16), pltpu.VMEM((D, fc), BF16)],
    )
    return pl.pallas_call(
        _moe_up_kernel, grid_spec=grid_spec,
        out_shape=jax.ShapeDtypeStruct((P, F), BF16),
        compiler_params=_cparams(2), name="moe_up")(block_e, n_active, xs, w_up_all, w_up_all)


def _moe_down_kernel(be_ref, nact_ref, h_ref, wd_ref, y_ref, wd_bf):
    i = pl.program_id(1)

    @pl.when(_new_expert(be_ref, i))
    def _():
        _cast_weight(wd_ref, wd_bf)

    @pl.when(i < nact_ref[0])
    def _():
        y_ref[...] = jnp.dot(h_ref[...], wd_bf[...], preferred_element_type=F32)

    @pl.when(i >= nact_ref[0])
    def _():
        y_ref[...] = jnp.zeros(y_ref.shape, y_ref.dtype)


def moe_down(h, w_down_all, layer, block_e, n_active, tn=2048):
    P, F = h.shape
    D = w_down_all.shape[-1]
    n_blocks = P // MOE_BLOCK
    grid_spec = pltpu.PrefetchScalarGridSpec(
        num_scalar_prefetch=2, grid=(D // tn, n_blocks),
        in_specs=[
            pl.BlockSpec((MOE_BLOCK, F), lambda j, i, be, na: (i, 0)),
            pl.BlockSpec((None, None, F, tn), lambda j, i, be, na: (layer, be[i], 0, j)),
        ],
        out_specs=pl.BlockSpec((MOE_BLOCK, tn), lambda j, i, be, na: (i, j)),
        scratch_shapes=[pltpu.VMEM((F, tn), BF16)],
    )
    return pl.pallas_call(
        _moe_down_kernel, grid_spec=grid_spec,
        out_shape=jax.ShapeDtypeStruct((P, D), F32),
        compiler_params=_cparams(2), name="moe_down")(block_e, n_active, h, w_down_all)


def _combine_kernel(pos_ref, pos_next_ref, x_ref, g_ref, gate_ref, y_ref, gn_ref, *rest,
                    tm, n_tiles, final, n_split):
    buf, sems = rest[-2:]
    i = pl.program_id(0)
    slot = i % 2

    def issue(idx_ref, s, unroll=4):
        def body(c, carry):
            for u in range(unroll):
                r = c * unroll + u
                for k in range(TOP_K):
                    _row_copy(y_ref, buf.at[s, k], idx_ref[0, 0, TOP_K * r + k], r,
                              sems.at[s]).start()
            return carry
        lax.fori_loop(0, tm // unroll, body, 0)

    @pl.when(i == 0)
    def _():
        issue(pos_ref, 0)

    @pl.when(i + 1 < n_tiles)
    def _():
        issue(pos_next_ref, 1 - slot)

    for k in range(TOP_K):
        _wait_rows(tm, y_ref, buf.at[slot, k], sems.at[slot])
    gate = gate_ref[...]
    moe = buf[slot, 0] * gate[:, 0:1] + buf[slot, 1] * gate[:, 1:2]
    xn = x_ref[...] + g_ref[0] * moe
    yn = _rms(xn, gn_ref[...])
    if final:
        ya_ref, yb_ref = rest[:2]

        @pl.when(i < n_split)
        def _():
            ya_ref[...] = yn

        @pl.when(i >= n_split)
        def _():
            yb_ref[...] = yn
    else:
        sc_ref, sh_ref, o_ref, u_ref = rest[:4]
        o_ref[...] = xn
        u_ref[...] = (yn * (1.0 + sc_ref[0]) + sh_ref[0]).astype(u_ref.dtype)


def moe_combine(x, y, pos, gate, mod3, g_chunk, row_fn, g_next, mod3_next=None, n_first=0, tm=128):
    M, D = x.shape
    n_tiles = M // tm
    final = mod3_next is None
    n_split = n_first // tm
    pos3 = pos.reshape(n_tiles, 1, TOP_K * tm)
    row_spec = pl.BlockSpec((tm, D), lambda i: (i, 0))
    in_specs = [
        pl.BlockSpec((1, 1, TOP_K * tm), lambda i: (i, 0, 0), memory_space=pltpu.SMEM),
        pl.BlockSpec((1, 1, TOP_K * tm), lambda i: (jnp.minimum(i + 1, n_tiles - 1), 0, 0),
                     memory_space=pltpu.SMEM),
        row_spec,
        pl.BlockSpec((1, 1, D), lambda i: (row_fn(i), 0, g_chunk)),
        pl.BlockSpec((tm, TOP_K), lambda i: (i, 0)),
        pl.BlockSpec(memory_space=pl.ANY),
        pl.BlockSpec((1, D), lambda i: (0, 0)),
    ]
    args = [pos3, pos3, x, mod3, gate, y, g_next.reshape(1, D)]
    scratch = [pltpu.VMEM((2, TOP_K, tm, D), F32), pltpu.SemaphoreType.DMA((2,))]
    kern = functools.partial(_combine_kernel, tm=tm, n_tiles=n_tiles, final=final, n_split=n_split)
    if final:
        out_specs = [pl.BlockSpec((tm, D), lambda i: (jnp.minimum(i, n_split - 1), 0)),
                     pl.BlockSpec((tm, D), lambda i: (jnp.maximum(i - n_split, 0), 0))]
        return pl.pallas_call(
            kern, grid=(n_tiles,), in_specs=in_specs, out_specs=out_specs,
            out_shape=[jax.ShapeDtypeStruct((n_first, D), F32),
                       jax.ShapeDtypeStruct((M - n_first, D), F32)],
            scratch_shapes=scratch, compiler_params=_cparams(1), name="moe_combine_final")(*args)
    in_specs += [pl.BlockSpec((1, 1, D), lambda i: (row_fn(i), 0, 1)),
                 pl.BlockSpec((1, 1, D), lambda i: (row_fn(i), 0, 0))]
    return pl.pallas_call(
        kern, grid=(n_tiles,), in_specs=in_specs, out_specs=[row_spec, row_spec],
        out_shape=[jax.ShapeDtypeStruct((M, D), F32), jax.ShapeDtypeStruct((M, D), BF16)],
        scratch_shapes=scratch, input_output_aliases={2: 0},
        compiler_params=_cparams(1), name="moe_combine")(*args, mod3_next, mod3_next)


def kernel(x_prompt, x_sample, cache_k, cache_v, state_rglru, c, c_ctx, w_ada, b_ada, norm_attn,
           norm_ffn, norm_final, w_in, w_pool, pool_scale, rpb, conv_w, conv_b, rg_w_a, rg_b_a,
           rg_w_i, rg_b_i, rg_lam, w_out, w_router, b_router, w_up, w_down):
    B, S, D = x_prompt.shape
    DB, T, _ = x_sample.shape
    L = w_in.shape[0]
    pool_w = w_pool.shape[1] * w_pool.shape[2]
    na_w = cache_k.shape[3] * cache_k.shape[4]
    rg_w = rg_lam.shape[-1]
    n_ctx = B * S
    n_tok = n_ctx + DB * T
    q_col0 = pool_w
    x_col0 = pool_w + 3 * na_w

    n_cond = 8
    cond = jnp.concatenate([c_ctx[None, :], c, jnp.zeros((n_cond - 1 - DB, D), F32)], axis=0)
    mod = ada_mod(cond, w_ada, b_ada)

    wr = jnp.zeros((D, 128), F32).at[:, :N_EXPERTS].set(w_router).astype(BF16)
    br = jnp.zeros((1, 128), F32).at[0, :N_EXPERTS].set(b_router)
    n_blocks = (n_tok * TOP_K) // MOE_BLOCK + N_EXPERTS

    ck4 = cache_k.reshape(DB, L, cache_k.shape[2], na_w)
    cv4 = cache_v.reshape(DB, L, cache_v.shape[2], na_w)
    rpb_flat = rpb.reshape(L, -1)
    rg_params = dict(conv_w=conv_w, conv_b=conv_b, rg_w_a=rg_w_a, rg_w_i=rg_w_i, rg_b_a=rg_b_a,
                     rg_b_i=rg_b_i, rg_lam=rg_lam)
    zeros_h = jnp.zeros((B, 2, rg_w), F32)

    new_k, new_v, new_h = [], [], []
    row256 = _cond_row_fn(256, n_ctx, T)
    row512 = _cond_row_fn(512, n_ctx, T)
    row128 = _cond_row_fn(128, n_ctx, T)
    mod3s = [mod[l].reshape(n_cond, 1, 6 * D) for l in range(L)]
    x, u = norm_mod_first(x_prompt.reshape(n_ctx, D), x_sample.reshape(DB * T, D), norm_attn[0],
                          mod3s[0], 1, 0, row256)
    for l in range(L):
        mod3 = mod3s[l]
        proj = mm_ws(u, w_in, l, tm=1024, tn=1024)

        kv = proj[:n_ctx, q_col0 + na_w:q_col0 + 3 * na_w]
        new_k.append(kv[:, :na_w].reshape(B, S, -1, NA_HEAD_DIM))
        new_v.append(kv[:, na_w:].reshape(B, S, -1, NA_HEAD_DIM))

        mix = pool_mixer(proj, w_pool, pool_scale, l, None, S, B, 0, D)
        mix = pool_mixer(proj, w_pool, pool_scale, l, mix, T, DB, n_ctx // T, D)
        mix = ctx_attention(proj, mix, S, B, na_w, q_col0)
        mix = natten(proj, ck4, cv4, rpb_flat, l, mix, T, DB, n_ctx // T, q_col0, na_w)
        mix, h_ctx = rglru(proj, rg_params, l, zeros_h, mix, S, B, 0, x_col0, pool_w + na_w)
        mix, _ = rglru(proj, rg_params, l, state_rglru[:, l], mix, T, DB, n_ctx // T, x_col0,
                       pool_w + na_w)
        new_h.append(h_ctx)

        x = mm_ws(mix, w_out, l, tm=512, tn=1024, res=x, gate=(mod3, 2, row512))

        u2, logits = norm_mod_router(x, norm_ffn[l], mod3, 4, 3, row256, (wr, br))
        slot_tok, gate, block_e, n_active, pos = route(logits[:, :N_EXPERTS], n_blocks)
        xs = gather_rows(u2, slot_tok, n_active, n_blocks)
        hmid = moe_up(xs, w_up, l, block_e, n_active)
        y = moe_down(hmid, w_down, l, block_e, n_active)
        if l + 1 < L:
            x, u = moe_combine(x, y, pos, gate, mod3, 5, row128, norm_attn[l + 1], mod3s[l + 1])
        else:
            y_ctx, y_lat = moe_combine(x, y, pos, gate, mod3, 5, row128, norm_final, n_first=n_ctx)

    return (y_ctx.reshape(B, S, D), y_lat.reshape(DB, T, D), jnp.stack(new_k, axis=1), jnp.stack(new_v, axis=1),
            jnp.stack(new_h, axis=1))
```

```python
import functools

import jax
import jax.numpy as jnp
from jax import lax
from jax.experimental import pallas as pl
from jax.experimental.pallas import tpu as pltpu

F32 = jnp.float32
BF16 = jnp.bfloat16

EPS = 1e-6
GRID_W = 64
POOL_WINDOWS = (2, 4, 8, 16)
N_POOL_GROUPS = 4
WIN_R = 8
WIN_C = 16
NA_HEAD_DIM = 128
RG_BLOCK_DIM = 128
CONV_W = 4
RG_C = 8.0
N_EXPERTS = 16
N_EXPERT_GROUPS = 4
GROUP_SIZE = N_EXPERTS // N_EXPERT_GROUPS
TOP_K = 2
MOE_BLOCK = 512

NEG_BIG = -1e30
LOG2E = 1.4426950408889634
VMEM_LIMIT_BYTES = 60 * 1024 * 1024

NAT_R = 4
NAT_NK = 12


def _cparams(n_axes):
    return pltpu.CompilerParams(
        dimension_semantics=("arbitrary",) * n_axes, vmem_limit_bytes=VMEM_LIMIT_BYTES)


def _silu(x):
    return x * jax.nn.sigmoid(x)


def _sigmoid_tanh(x):
    return 0.5 * jnp.tanh(0.5 * x) + 0.5


def _ada_kernel(c_ref, w_ref, b_ref, o_ref):
    s = _silu(c_ref[...]).astype(BF16)
    w = w_ref[...].astype(BF16)
    o_ref[...] = jnp.dot(s, w, preferred_element_type=F32) + b_ref[...]


def ada_mod(cond, w_ada, b_ada, tn=512):
    L, D, N = w_ada.shape
    R = cond.shape[0]
    return pl.pallas_call(
        _ada_kernel,
        grid=(L, N // tn),
        in_specs=[
            pl.BlockSpec((R, D), lambda l, j: (0, 0)),
            pl.BlockSpec((None, D, tn), lambda l, j: (l, 0, j)),
            pl.BlockSpec((None, 1, tn), lambda l, j: (l, 0, j)),
        ],
        out_specs=pl.BlockSpec((None, R, tn), lambda l, j: (l, 0, j)),
        out_shape=jax.ShapeDtypeStruct((L, R, N), F32),
        compiler_params=_cparams(2),
        name="ada_mod",
    )(cond, w_ada, b_ada.reshape(L, 1, N))


def _rms(x, g):
    ms = jnp.mean(x * x, axis=-1, keepdims=True)
    return x * lax.rsqrt(ms + EPS) * g


HI16 = 0xFFFF0000


def _pack_bf16_pairs(ub):
    half = ub.shape[1] // 2
    bits = lax.bitcast_convert_type(ub.astype(F32), jnp.uint32)
    return (bits[:, half:] & jnp.uint32(HI16)) | (bits[:, :half] >> 16)


def _unpack_bf16_pairs(w):
    lo = lax.bitcast_convert_type(w << 16, F32).astype(BF16)
    hi = lax.bitcast_convert_type(w & jnp.uint32(HI16), F32).astype(BF16)
    return lo, hi


def _norm_mod_router_kernel(x_ref, g_ref, sc_ref, sh_ref, wr_ref, br_ref, u_ref, lg_ref):
    y = _rms(x_ref[...], g_ref[...])
    ub = (y * (1.0 + sc_ref[0]) + sh_ref[0]).astype(BF16)
    u_ref[...] = _pack_bf16_pairs(ub)
    lg_ref[...] = jnp.dot(ub, wr_ref[...], preferred_element_type=F32) + br_ref[...]


def _norm_mod_first_kernel(xa_ref, xb_ref, g_ref, sc_ref, sh_ref, x_ref, u_ref, *, n_a):
    x = jnp.where(pl.program_id(0) < n_a, xa_ref[...], xb_ref[...])
    x_ref[...] = x
    u_ref[...] = (_rms(x, g_ref[...]) * (1.0 + sc_ref[0]) + sh_ref[0]).astype(u_ref.dtype)


def norm_mod_first(xa, xb, g, mod3, sc_chunk, sh_chunk, row_fn, tm=256):
    D = xa.shape[1]
    n_a, n_b = xa.shape[0] // tm, xb.shape[0] // tm
    M = xa.shape[0] + xb.shape[0]
    row_spec = pl.BlockSpec((tm, D), lambda i: (i, 0))
    return pl.pallas_call(
        functools.partial(_norm_mod_first_kernel, n_a=n_a), grid=(n_a + n_b,),
        in_specs=[
            pl.BlockSpec((tm, D), lambda i: (jnp.minimum(i, n_a - 1), 0)),
            pl.BlockSpec((tm, D), lambda i: (jnp.maximum(i - n_a, 0), 0)),
            pl.BlockSpec((1, D), lambda i: (0, 0)),
            pl.BlockSpec((1, 1, D), lambda i: (row_fn(i), 0, sc_chunk)),
            pl.BlockSpec((1, 1, D), lambda i: (row_fn(i), 0, sh_chunk)),
        ],
        out_specs=[row_spec, row_spec],
        out_shape=[jax.ShapeDtypeStruct((M, D), F32), jax.ShapeDtypeStruct((M, D), BF16)],
        compiler_params=_cparams(1), name="norm_mod_first")(xa, xb, g.reshape(1, D), mod3, mod3)


def _cond_row_fn(tm, n_ctx_rows, dec_seq):
    def row(i):
        r0 = i * tm
        return jnp.where(r0 < n_ctx_rows, 0, 1 + (r0 - n_ctx_rows) // dec_seq)
    return row


def norm_mod_router(x, g, mod3, sc_chunk, sh_chunk, row_fn, router, tm=256):
    M, D = x.shape
    wr, br = router
    NE = wr.shape[1]
    in_specs = [
        pl.BlockSpec((tm, D), lambda i: (i, 0)),
        pl.BlockSpec((1, D), lambda i: (0, 0)),
        pl.BlockSpec((1, 1, D), lambda i: (row_fn(i), 0, sc_chunk)),
        pl.BlockSpec((1, 1, D), lambda i: (row_fn(i), 0, sh_chunk)),
        pl.BlockSpec((D, NE), lambda i: (0, 0)),
        pl.BlockSpec((1, NE), lambda i: (0, 0)),
    ]
    return pl.pallas_call(
        _norm_mod_router_kernel, grid=(M // tm,), in_specs=in_specs,
        out_specs=[pl.BlockSpec((tm, D // 2), lambda i: (i, 0)),
                   pl.BlockSpec((tm, NE), lambda i: (i, 0))],
        out_shape=[jax.ShapeDtypeStruct((M, D // 2), jnp.uint32),
                   jax.ShapeDtypeStruct((M, NE), F32)],
        compiler_params=_cparams(1), name="norm_mod_router")(x, g.reshape(1, D), mod3, mod3, wr, br)


def _cast_weight(w_ref, wbf_ref, kchunk=512):
    K = w_ref.shape[0]
    for k0 in range(0, K, kchunk):
        k1 = min(K, k0 + kchunk)
        wbf_ref[k0:k1, :] = w_ref[k0:k1, :].astype(BF16)


def _mm_kernel(lhs_ref, w_ref, o_ref, wbf_ref):
    @pl.when(pl.program_id(1) == 0)
    def _():
        _cast_weight(w_ref, wbf_ref)
    o_ref[...] = jnp.dot(lhs_ref[...], wbf_ref[...], preferred_element_type=F32).astype(o_ref.dtype)


def _mm_res_kernel(lhs_ref, w_ref, res_ref, g_ref, o_ref, wbf_ref):
    @pl.when(pl.program_id(1) == 0)
    def _():
        _cast_weight(w_ref, wbf_ref)
    acc = jnp.dot(lhs_ref[...], wbf_ref[...], preferred_element_type=F32)
    o_ref[...] = res_ref[...] + g_ref[0] * acc


def mm_ws(lhs, w_all, layer, tm, tn, res=None, gate=None, out_dtype=F32):
    M, K = lhs.shape
    N = w_all.shape[-1]
    in_specs = [
        pl.BlockSpec((tm, K), lambda j, i: (i, 0)),
        pl.BlockSpec((None, K, tn), lambda j, i: (layer, 0, j), pipeline_mode=pl.Buffered(1)),
    ]
    out_spec = pl.BlockSpec((tm, tn), lambda j, i: (i, j))
    scratch = [pltpu.VMEM((K, tn), BF16)]
    if res is None:
        return pl.pallas_call(
            _mm_kernel, grid=(N // tn, M // tm), in_specs=in_specs, out_specs=out_spec,
            out_shape=jax.ShapeDtypeStruct((M, N), out_dtype), scratch_shapes=scratch,
            compiler_params=_cparams(2), name="mm_ws")(lhs, w_all)
    mod3, g_chunk, row_fn = gate
    in_specs += [
        pl.BlockSpec((tm, tn), lambda j, i: (i, j)),
        pl.BlockSpec((1, 1, tn), lambda j, i: (row_fn(i), 0, g_chunk * (N // tn) + j)),
    ]
    return pl.pallas_call(
        _mm_res_kernel, grid=(N // tn, M // tm), in_specs=in_specs, out_specs=out_spec,
        out_shape=jax.ShapeDtypeStruct((M, N), F32), scratch_shapes=scratch,
        input_output_aliases={2: 0},
        compiler_params=_cparams(2), name="mm_ws_res")(lhs, w_all, res, mod3)


POOL_PAD = 16


def _pool_kernel(x_ref, w_ref, s_ref, *rest, T, CH):
    o_ref, xpad = rest[-2], rest[-1]
    g = pl.program_id(1)
    C = x_ref.shape[1]
    xpad[0:POOL_PAD, :] = jnp.zeros((POOL_PAD, C), F32)
    xpad[T + POOL_PAD:T + 2 * POOL_PAD, :] = jnp.zeros((POOL_PAD, C), F32)
    for c in range(T // CH):
        xpad[POOL_PAD + c * CH:POOL_PAD + (c + 1) * CH, :] = x_ref[c * CH:(c + 1) * CH, :]
    wbf = w_ref[...].astype(BF16)
    scale = s_ref[...]
    for gi, win in enumerate(POOL_WINDOWS):
        half = win // 2

        @pl.when(g == gi)
        def _():
            for c in range(T // CH):
                base = c * CH
                acc = xpad[base + POOL_PAD - half:base + POOL_PAD - half + CH, :]
                for d in range(-half + 1, half):
                    acc = acc + xpad[base + POOL_PAD + d:base + POOL_PAD + d + CH, :]
                t = base + lax.broadcasted_iota(jnp.int32, (CH, 1), 0)
                cnt = (jnp.minimum(t + half, T) - jnp.maximum(t - half, 0)).astype(F32)
                xc = xpad[base + POOL_PAD:base + POOL_PAD + CH, :]
                pooled = (acc / cnt - xc).astype(BF16)
                y = jnp.dot(pooled, wbf, preferred_element_type=F32) * scale
                o_ref[base:base + CH, :] = y.astype(o_ref.dtype)


def pool_mixer(proj, w_pool_all, pool_scale_all, layer, mix, T, n_seq, row_blk0, mix_cols):
    C = w_pool_all.shape[-1]
    W = N_POOL_GROUPS * C
    in_specs = [
        pl.BlockSpec((T, C), lambda b, g: (row_blk0 + b, g)),
        pl.BlockSpec((None, None, C, C), lambda b, g: (layer, g, 0, 0)),
        pl.BlockSpec((None, 1, C), lambda b, g: (layer, 0, g)),
    ]
    args = [proj, w_pool_all, pool_scale_all.reshape(-1, 1, W)]
    aliases = {}
    if mix is not None:
        in_specs.append(pl.BlockSpec(memory_space=pl.ANY))
        args.append(mix)
        aliases = {3: 0}
    return pl.pallas_call(
        functools.partial(_pool_kernel, T=T, CH=min(T, 256)),
        grid=(n_seq, N_POOL_GROUPS), in_specs=in_specs,
        out_specs=pl.BlockSpec((T, C), lambda b, g: (row_blk0 + b, g)),
        out_shape=jax.ShapeDtypeStruct((proj.shape[0], mix_cols), BF16),
        scratch_shapes=[pltpu.VMEM((T + 2 * POOL_PAD, C), F32)],
        input_output_aliases=aliases,
        compiler_params=_cparams(2), name=f"pool_T{T}")(*args)


def _ctx_attn_kernel(q_ref, k_ref, v_ref, mix_ref, o_ref, *, n_heads):
    del mix_ref
    Dh = NA_HEAD_DIM
    scale = Dh ** -0.5
    for h in range(n_heads):
        sl = slice(h * Dh, (h + 1) * Dh)
        q = q_ref[:, sl].astype(BF16)
        k = k_ref[:, sl].astype(BF16)
        v = v_ref[:, sl].astype(BF16)
        s = lax.dot_general(q, k, (((1,), (1,)), ((), ())), preferred_element_type=F32) * scale
        m = jnp.max(s, axis=-1, keepdims=True)
        p = jnp.exp(s - m)
        l = jnp.sum(p, axis=-1, keepdims=True)
        o = jnp.dot(p.astype(BF16), v, preferred_element_type=F32)
        o_ref[:, sl] = (o * (1.0 / l)).astype(o_ref.dtype)


def ctx_attention(proj, mix, S, n_seq, na_width, q_col0):
    cb = 1024
    nhalf = na_width // cb
    qb, kb, vb = q_col0 // cb, (q_col0 + na_width) // cb, (q_col0 + 2 * na_width) // cb
    return pl.pallas_call(
        functools.partial(_ctx_attn_kernel, n_heads=cb // NA_HEAD_DIM),
        grid=(n_seq, nhalf),
        in_specs=[
            pl.BlockSpec((S, cb), lambda b, c: (b, qb + c)),
            pl.BlockSpec((S, cb), lambda b, c: (b, kb + c)),
            pl.BlockSpec((S, cb), lambda b, c: (b, vb + c)),
            pl.BlockSpec(memory_space=pl.ANY),
        ],
        out_specs=pl.BlockSpec((S, cb), lambda b, c: (b, qb + c)),
        out_shape=jax.ShapeDtypeStruct(mix.shape, mix.dtype),
        input_output_aliases={3: 0},
        compiler_params=_cparams(2), name="ctx_attn")(proj, proj, proj, mix)


def _natten_cases(rows):
    R, NK = NAT_R, NAT_NK
    cases = []
    for r0 in (0, R, rows - R):
        start = min(max(r0 - WIN_R // 2, 0), rows - NK)
        tiles = []
        for a in range(R):
            r = r0 + a
            rs = min(max(r - WIN_R // 2, 0), rows - WIN_R)
            row = []
            for i in range(NK):
                kr = start + i
                row.append(kr - r + (WIN_R - 1) if rs <= kr < rs + WIN_R else None)
            tiles.append(row)
        cases.append((start - r0, tiles))
    return cases


def _natten_kernel(rpb_ref, q_ref, k_ref, v_ref, ck_ref, cv_ref, mix_ref, o_ref,
                   qbf, kbf, vbf, ckbf, cvbf, tb, bm, *, rows):
    del mix_ref
    W = GRID_W
    R, NK = NAT_R, NAT_NK
    RQ, NKK = R * W, NK * W
    Dh = NA_HEAD_DIM
    scale2 = Dh ** -0.5 * LOG2E
    h = pl.program_id(1)
    n_dr, n_dc = 2 * WIN_R - 1, 2 * WIN_C - 1
    T = rows * W

    for c0 in range(0, T, 512):
        qbf[c0:c0 + 512, :] = q_ref[c0:c0 + 512, :].astype(BF16)
        kbf[c0:c0 + 512, :] = k_ref[c0:c0 + 512, :].astype(BF16)
        vbf[c0:c0 + 512, :] = v_ref[c0:c0 + 512, :].astype(BF16)
    ckbf[...] = ck_ref[...].astype(BF16)
    cvbf[...] = cv_ref[...].astype(BF16)

    ci = lax.broadcasted_iota(jnp.int32, (W, W), 0)
    kci = lax.broadcasted_iota(jnp.int32, (W, W), 1)
    dci = kci - ci + (WIN_C - 1)
    cs = jnp.clip(ci - WIN_C // 2, 0, W - WIN_C)
    in_win = (kci >= cs) & (kci < cs + WIN_C)
    for dr in range(n_dr):
        t = jnp.full((W, W), NEG_BIG, F32)
        for dc in range(n_dc):
            t = jnp.where(dci == dc, rpb_ref[h * (n_dr * n_dc) + dr * n_dc + dc] * LOG2E, t)
        tb[dr] = jnp.where(in_win, t, NEG_BIG)

    cases = _natten_cases(rows)
    neg_tile = jnp.full((W, W), NEG_BIG, F32)
    for ci_, (_, tiles) in enumerate(cases):
        for a in range(R):
            for i in range(0, NK, 2):
                pair = [tb[d] if d is not None else neg_tile for d in tiles[a][i:i + 2]]
                bm[ci_, a * W:(a + 1) * W, i * W:(i + 2) * W] = jnp.concatenate(pair, axis=1)

    nt = (((1,), (1,)), ((), ()))

    def block(q0, k0, case):
        qb = qbf[pl.ds(q0, RQ), :]
        kk = kbf[pl.ds(k0, NKK), :]
        vv = vbf[pl.ds(k0, NKK), :]
        s1 = lax.dot_general(qb, kk, nt, preferred_element_type=F32) * scale2 + bm[case]
        s2 = lax.dot_general(qb, ckbf[...], nt, preferred_element_type=F32) * scale2
        m = jnp.maximum(jnp.max(s1, axis=-1, keepdims=True), jnp.max(s2, axis=-1, keepdims=True))
        p1 = jnp.exp2(s1 - m)
        p2 = jnp.exp2(s2 - m)
        l = jnp.sum(p1, axis=-1, keepdims=True) + jnp.sum(p2, axis=-1, keepdims=True)
        o = (jnp.dot(p1.astype(BF16), vv, preferred_element_type=F32)
             + jnp.dot(p2.astype(BF16), cvbf[...], preferred_element_type=F32))
        o_ref[pl.ds(q0, RQ), :] = (o * (1.0 / l)).astype(o_ref.dtype)

    block(0, (cases[0][0]) * W, 0)

    for blk in range(1, rows // R - 1):
        block(blk * RQ, (blk * R + cases[1][0]) * W, 1)
    q_last = (rows - R) * W
    block(q_last, q_last + cases[2][0] * W, 2)


def natten(proj, cache_k, cache_v, rpb_all, layer, mix, T, n_seq, row_blk0, q_col0, na_width):
    Dh = NA_HEAD_DIM
    H = na_width // Dh
    P = cache_k.shape[2]
    rows = T // GRID_W
    qb, kb, vb = q_col0 // Dh, (q_col0 + na_width) // Dh, (q_col0 + 2 * na_width) // Dh
    return pl.pallas_call(
        functools.partial(_natten_kernel, rows=rows),
        grid=(n_seq, H),
        in_specs=[
            pl.BlockSpec(memory_space=pltpu.SMEM),
            pl.BlockSpec((T, Dh), lambda b, h: (row_blk0 + b, qb + h)),
            pl.BlockSpec((T, Dh), lambda b, h: (row_blk0 + b, kb + h)),
            pl.BlockSpec((T, Dh), lambda b, h: (row_blk0 + b, vb + h)),
            pl.BlockSpec((None, None, P, Dh), lambda b, h: (b, layer, 0, h)),
            pl.BlockSpec((None, None, P, Dh), lambda b, h: (b, layer, 0, h)),
            pl.BlockSpec(memory_space=pl.ANY),
        ],
        out_specs=pl.BlockSpec((T, Dh), lambda b, h: (row_blk0 + b, qb + h)),
        out_shape=jax.ShapeDtypeStruct(mix.shape, mix.dtype),
        scratch_shapes=[
            pltpu.VMEM((T, Dh), BF16), pltpu.VMEM((T, Dh), BF16), pltpu.VMEM((T, Dh), BF16),
            pltpu.VMEM((P, Dh), BF16), pltpu.VMEM((P, Dh), BF16),
            pltpu.VMEM((2 * WIN_R - 1, GRID_W, GRID_W), F32),
            pltpu.VMEM((3, NAT_R * GRID_W, NAT_NK * GRID_W), F32),
        ],
        input_output_aliases={6: 0},
        compiler_params=_cparams(2), name="natten")(
            rpb_all[layer], proj, proj, proj, cache_k, cache_v, mix)


RG_PAD = 8
SCAN_ROWS = 8


def _scan8(a, b, reverse):
    row = lax.broadcasted_iota(jnp.int32, a.shape, 0)
    for s in (1, 2, 4):
        if reverse:
            sh, keep = SCAN_ROWS - s, row < SCAN_ROWS - s
        else:
            sh, keep = s, row >= s
        a_s = pltpu.roll(a, sh, 0)
        b_s = pltpu.roll(b, sh, 0)
        b = jnp.where(keep, a * b_s + b, b)
        a = jnp.where(keep, a * a_s, a)
    return a, b


def _gelu_tanh(x):
    return 0.5 * x * (1.0 + jnp.tanh(0.7978845608028654 * (x + 0.044715 * (x * x * x))))


def _rglru_kernel(x_ref, gt_ref, cw_ref, cb_ref, wa_ref, wi_ref, ba_ref, bi_ref, lam_ref, h0_ref,
                  *rest, T, CH, U):
    y_ref, hs_ref, xpad, a_f, b_f, a_b, b_b, hf_s, hb_s = rest[-9:]
    C = x_ref.shape[1]
    xpad[0:RG_PAD, :] = jnp.zeros((RG_PAD, C), F32)
    xpad[T + RG_PAD:T + 2 * RG_PAD, :] = jnp.zeros((RG_PAD, C), F32)
    for c in range(T // CH):
        xpad[RG_PAD + c * CH:RG_PAD + (c + 1) * CH, :] = x_ref[c * CH:(c + 1) * CH, :]

    wcat = jnp.concatenate([wa_ref[0], wa_ref[1], wi_ref[0], wi_ref[1]], axis=1).astype(BF16)
    bcat = jnp.concatenate([ba_ref[0], ba_ref[1], bi_ref[0], bi_ref[1]], axis=1)
    z = -lam_ref[...]
    sp = jnp.maximum(z, 0.0) + jnp.log1p(jnp.exp(-jnp.abs(z)))
    cw = cw_ref[...]
    cb = cb_ref[...]
    lead = RG_PAD - CONV_W // 2

    def gates(c, carry):
        c0 = pl.multiple_of(c * CH, CH)
        win = xpad[pl.ds(c0, CH + 2 * RG_PAD), :]
        xc = cb
        for k in range(CONV_W):
            xc = xc + win[lead + k:lead + k + CH, :] * cw[k:k + 1, :]
        gz = jnp.dot(xc.astype(BF16), wcat, preferred_element_type=F32) + bcat
        for d, (a_s, b_s) in enumerate(((a_f, b_f), (a_b, b_b))):
            r = _sigmoid_tanh(gz[:, d * C:(d + 1) * C])
            ig = _sigmoid_tanh(gz[:, (2 + d) * C:(3 + d) * C])
            log_a = -RG_C * r * sp[d]
            a = jnp.exp(log_a)
            a_s[pl.ds(c0, CH), :] = a
            b_s[pl.ds(c0, CH), :] = jnp.sqrt(1.0 - a * a) * ig * xc
        return carry

    lax.fori_loop(0, T // CH, gates, 0)

    RU = SCAN_ROWS * U
    n_it = T // RU

    def scan(j, carry):
        hf, hb = carry
        r0 = pl.multiple_of(j * RU, RU)
        r1 = pl.multiple_of((n_it - 1 - j) * RU, RU)
        af, bf = a_f[pl.ds(r0, RU), :], b_f[pl.ds(r0, RU), :]
        ab, bb = a_b[pl.ds(r1, RU), :], b_b[pl.ds(r1, RU), :]
        outs_f, outs_b = [None] * U, [None] * U
        for u in range(U):
            a8, b8 = _scan8(af[u * 8:(u + 1) * 8], bf[u * 8:(u + 1) * 8], False)
            rows_f = b8 + a8 * hf
            outs_f[u] = rows_f
            hf = jnp.broadcast_to(rows_f[SCAN_ROWS - 1:SCAN_ROWS, :], rows_f.shape)
            v = U - 1 - u
            a8, b8 = _scan8(ab[v * 8:(v + 1) * 8], bb[v * 8:(v + 1) * 8], True)
            rows_b = b8 + a8 * hb
            outs_b[v] = rows_b
            hb = jnp.broadcast_to(rows_b[0:1, :], rows_b.shape)
        hf_s[pl.ds(r0, RU), :] = jnp.concatenate(outs_f, axis=0)
        hb_s[pl.ds(r1, RU), :] = jnp.concatenate(outs_b, axis=0)
        return hf, hb

    h0 = h0_ref[...]
    hf_last, hb_first = lax.fori_loop(
        0, n_it, scan, (jnp.broadcast_to(h0[0:1, :], (SCAN_ROWS, C)),
                        jnp.broadcast_to(h0[1:2, :], (SCAN_ROWS, C))))
    hs_ref[0:1, :] = hf_last[SCAN_ROWS - 1:SCAN_ROWS, :]
    hs_ref[1:2, :] = hb_first[0:1, :]

    def emit(c, carry):
        c0 = pl.multiple_of(c * CH, CH)
        h = hf_s[pl.ds(c0, CH), :] + hb_s[pl.ds(c0, CH), :]
        y_ref[pl.ds(c0, CH), :] = (h * _gelu_tanh(gt_ref[pl.ds(c0, CH), :])).astype(y_ref.dtype)
        return carry

    lax.fori_loop(0, T // CH, emit, 0)


def rglru(proj, p, layer, h0, mix, T, n_seq, row_blk0, x_col0, mix_col0):
    C = RG_BLOCK_DIM
    W = h0.shape[-1]
    nb = W // C
    xb, gb, mb = x_col0 // C, (x_col0 + W) // C, mix_col0 // C
    L = p['conv_w'].shape[0]
    vec = lambda arr: arr.reshape(L, -1, 1, W)
    in_specs = [
        pl.BlockSpec((T, C), lambda b, n: (row_blk0 + b, xb + n)),
        pl.BlockSpec((T, C), lambda b, n: (row_blk0 + b, gb + n)),
        pl.BlockSpec((None, CONV_W, C), lambda b, n: (layer, 0, n)),
        pl.BlockSpec((None, 1, C), lambda b, n: (layer, 0, n)),
        pl.BlockSpec((None, 2, None, C, C), lambda b, n: (layer, 0, n, 0, 0)),
        pl.BlockSpec((None, 2, None, C, C), lambda b, n: (layer, 0, n, 0, 0)),
        pl.BlockSpec((None, 2, 1, C), lambda b, n: (layer, 0, 0, n)),
        pl.BlockSpec((None, 2, 1, C), lambda b, n: (layer, 0, 0, n)),
        pl.BlockSpec((None, 2, 1, C), lambda b, n: (layer, 0, 0, n)),
        pl.BlockSpec((None, 2, C), lambda b, n: (b, 0, n)),
        pl.BlockSpec(memory_space=pl.ANY),
    ]
    args = [proj, proj, p['conv_w'], p['conv_b'].reshape(L, 1, W), p['rg_w_a'], p['rg_w_i'],
            vec(p['rg_b_a']), vec(p['rg_b_i']), vec(p['rg_lam']), h0, mix]
    return pl.pallas_call(
        functools.partial(_rglru_kernel, T=T, CH=min(T, 256), U=4),
        grid=(n_seq, nb), in_specs=in_specs,
        out_specs=[
            pl.BlockSpec((T, C), lambda b, n: (row_blk0 + b, mb + n)),
            pl.BlockSpec((None, 2, C), lambda b, n: (b, 0, n)),
        ],
        out_shape=[jax.ShapeDtypeStruct(mix.shape, mix.dtype),
                   jax.ShapeDtypeStruct((n_seq, 2, W), F32)],
        scratch_shapes=[pltpu.VMEM((T + 2 * RG_PAD, C), F32)] + [pltpu.VMEM((T, C), F32)] * 6,
        input_output_aliases={10: 0},
        compiler_params=_cparams(2), name=f"rglru_T{T}")(*args)


def route(logits, n_blocks):
    N = logits.shape[0]
    lg = logits.reshape(N, N_EXPERT_GROUPS, GROUP_SIZE)
    lane = jnp.arange(GROUP_SIZE, dtype=jnp.int32)
    i1 = jnp.argmax(lg, axis=-1).astype(jnp.int32)
    v1 = jnp.max(lg, axis=-1)
    rest = jnp.where(lane == i1[..., None], -jnp.inf, lg)
    i2 = jnp.argmax(rest, axis=-1).astype(jnp.int32)
    v2 = jnp.max(rest, axis=-1)
    g_sel = jnp.argmax(v1 + v2, axis=-1).astype(jnp.int32)
    pick = lambda a: jnp.take_along_axis(a, g_sel[:, None], axis=1)[:, 0]
    sel_v = jnp.stack([pick(v1), pick(v2)], axis=-1)
    sel_i = jnp.stack([pick(i1), pick(i2)], axis=-1)
    expert = g_sel[:, None] * GROUP_SIZE + sel_i
    gate = jax.nn.softmax(sel_v, axis=-1)

    A = N * TOP_K
    flat_e = expert.reshape(A)
    onehot = (flat_e[:, None] == jnp.arange(N_EXPERTS, dtype=jnp.int32)[None, :]).astype(F32)
    ch = 256
    oh3 = onehot.reshape(A // ch, ch, N_EXPERTS)
    tri = (jnp.arange(ch)[:, None] > jnp.arange(ch)[None, :]).astype(F32)
    local = jnp.einsum('ij,bjk->bik', tri, oh3, precision=lax.Precision.HIGHEST)
    chunk_sum = oh3.sum(axis=1)
    chunk_off = jnp.cumsum(chunk_sum, axis=0) - chunk_sum
    rank = jnp.sum((local + chunk_off[:, None, :]) * oh3, axis=-1).reshape(A).astype(jnp.int32)
    counts = chunk_sum.sum(axis=0).astype(jnp.int32)
    padded = ((counts + MOE_BLOCK - 1) // MOE_BLOCK) * MOE_BLOCK
    pad_end = jnp.cumsum(padded)
    pad_start = pad_end - padded
    dest = (jnp.sum(onehot * pad_start.astype(F32)[None, :], axis=-1).astype(jnp.int32) + rank)
    P = n_blocks * MOE_BLOCK
    flat_tok = jnp.arange(A, dtype=jnp.int32) // TOP_K
    slot_tok = jnp.zeros((P,), jnp.int32).at[dest].set(flat_tok)
    blk_start = jnp.arange(n_blocks, dtype=jnp.int32) * MOE_BLOCK
    block_e = jnp.minimum(jnp.sum((pad_end[None, :] <= blk_start[:, None]).astype(jnp.int32), axis=1),
                          N_EXPERTS - 1).astype(jnp.int32)
    n_active = (pad_end[-1] // MOE_BLOCK).astype(jnp.int32).reshape(1)
    return slot_tok, gate, block_e, n_active, dest.reshape(N, TOP_K)


def _row_copy(src, dst, s, d, sem):
    return pltpu.make_async_copy(src.at[pl.ds(s, 1)], dst.at[pl.ds(d, 1)], sem)


def _issue_rows(idx_ref, n_rows, src_ref, dst_ref, sem, unroll=8):
    def issue(c, carry):
        for u in range(unroll):
            r = c * unroll + u
            _row_copy(src_ref, dst_ref, idx_ref[0, 0, r], r, sem).start(priority=u % 2)
        return carry
    lax.fori_loop(0, n_rows // unroll, issue, 0)


def _wait_rows(n_rows, src_ref, dst_ref, sem):
    pltpu.make_async_copy(src_ref.at[pl.ds(0, n_rows)], dst_ref, sem).wait()


def _gather_kernel(nact_ref, idx_ref, idx_next_ref, src_ref, dst_ref, buf, sems):
    i = pl.program_id(0)
    nact = nact_ref[0]
    slot = i % 2

    @pl.when(i == 0)
    def _():
        _issue_rows(idx_ref, MOE_BLOCK, src_ref, buf.at[0], sems.at[0])

    @pl.when(i + 1 < nact)
    def _():
        _issue_rows(idx_next_ref, MOE_BLOCK, src_ref, buf.at[1 - slot], sems.at[1 - slot])

    @pl.when(i < nact)
    def _():
        _wait_rows(MOE_BLOCK, src_ref, buf.at[slot], sems.at[slot])
        half = src_ref.shape[1]
        lo, hi = _unpack_bf16_pairs(buf[slot])
        dst_ref[:, :half] = lo
        dst_ref[:, half:] = hi

    @pl.when(i >= nact)
    def _():
        dst_ref[...] = jnp.zeros(dst_ref.shape, dst_ref.dtype)


def gather_rows(src, slot_tok, n_active, n_blocks):
    Dp = src.shape[1]
    D = 2 * Dp
    last = n_blocks - 1
    grid_spec = pltpu.PrefetchScalarGridSpec(
        num_scalar_prefetch=1, grid=(n_blocks,),
        in_specs=[
            pl.BlockSpec((1, 1, MOE_BLOCK), lambda i, na: (i, 0, 0), memory_space=pltpu.SMEM),
            pl.BlockSpec((1, 1, MOE_BLOCK), lambda i, na: (jnp.minimum(i + 1, last), 0, 0),
                         memory_space=pltpu.SMEM),
            pl.BlockSpec(memory_space=pl.ANY),
        ],
        out_specs=pl.BlockSpec((MOE_BLOCK, D), lambda i, na: (i, 0)),
        scratch_shapes=[pltpu.VMEM((2, MOE_BLOCK, Dp), src.dtype), pltpu.SemaphoreType.DMA((2,))],
    )
    idx3 = slot_tok.reshape(n_blocks, 1, MOE_BLOCK)
    return pl.pallas_call(
        _gather_kernel, grid_spec=grid_spec,
        out_shape=jax.ShapeDtypeStruct((n_blocks * MOE_BLOCK, D), BF16),
        compiler_params=_cparams(1), name="moe_gather")(n_active, idx3, idx3, src)


def _new_expert(be_ref, i):
    return jnp.logical_or(i == 0, be_ref[i] != be_ref[jnp.maximum(i - 1, 0)])


def _moe_up_kernel(be_ref, nact_ref, xs_ref, wg_ref, wu_ref, h_ref, wg_bf, wu_bf):
    i = pl.program_id(1)

    @pl.when(_new_expert(be_ref, i))
    def _():
        _cast_weight(wg_ref, wg_bf)
        _cast_weight(wu_ref, wu_bf)

    @pl.when(i < nact_ref[0])
    def _():
        x = xs_ref[...]
        hg = jnp.dot(x, wg_bf[...], preferred_element_type=F32)
        hu = jnp.dot(x, wu_bf[...], preferred_element_type=F32)
        h_ref[...] = (_silu(hg) * hu).astype(h_ref.dtype)

    @pl.when(i >= nact_ref[0])
    def _():
        h_ref[...] = jnp.zeros(h_ref.shape, h_ref.dtype)


def moe_up(xs, w_up_all, layer, block_e, n_active, fc=512):
    P, D = xs.shape
    F = w_up_all.shape[-1] // 2
    n_blocks = P // MOE_BLOCK
    nfc = F // fc
    grid_spec = pltpu.PrefetchScalarGridSpec(
        num_scalar_prefetch=2, grid=(nfc, n_blocks),
        in_specs=[
            pl.BlockSpec((MOE_BLOCK, D), lambda j, i, be, na: (i, 0)),
            pl.BlockSpec((None, None, D, fc), lambda j, i, be, na: (layer, be[i], 0, j)),
            pl.BlockSpec((None, None, D, fc), lambda j, i, be, na: (layer, be[i], 0, nfc + j)),
        ],
        out_specs=pl.BlockSpec((MOE_BLOCK, fc), lambda j, i, be, na: (i, j)),
        scratch_shapes=[pltpu.VMEM((D, fc), BF16), pltpu.VMEM((D, fc), BF16)],
    )
    return pl.pallas_call(
        _moe_up_kernel, grid_spec=grid_spec,
        out_shape=jax.ShapeDtypeStruct((P, F), BF16),
        compiler_params=_cparams(2), name="moe_up")(block_e, n_active, xs, w_up_all, w_up_all)


def _moe_down_kernel(be_ref, nact_ref, h_ref, wd_ref, y_ref, wd_bf):
    i = pl.program_id(1)

    @pl.when(_new_expert(be_ref, i))
    def _():
        _cast_weight(wd_ref, wd_bf)

    @pl.when(i < nact_ref[0])
    def _():
        y_ref[...] = jnp.dot(h_ref[...], wd_bf[...], preferred_element_type=F32)

    @pl.when(i >= nact_ref[0])
    def _():
        y_ref[...] = jnp.zeros(y_ref.shape, y_ref.dtype)


def moe_down(h, w_down_all, layer, block_e, n_active, tn=2048):
    P, F = h.shape
    D = w_down_all.shape[-1]
    n_blocks = P // MOE_BLOCK
    grid_spec = pltpu.PrefetchScalarGridSpec(
        num_scalar_prefetch=2, grid=(D // tn, n_blocks),
        in_specs=[
            pl.BlockSpec((MOE_BLOCK, F), lambda j, i, be, na: (i, 0)),
            pl.BlockSpec((None, None, F, tn), lambda j, i, be, na: (layer, be[i], 0, j)),
        ],
        out_specs=pl.BlockSpec((MOE_BLOCK, tn), lambda j, i, be, na: (i, j)),
        scratch_shapes=[pltpu.VMEM((F, tn), BF16)],
    )
    return pl.pallas_call(
        _moe_down_kernel, grid_spec=grid_spec,
        out_shape=jax.ShapeDtypeStruct((P, D), F32),
        compiler_params=_cparams(2), name="moe_down")(block_e, n_active, h, w_down_all)


def _combine_kernel(pos_ref, pos_next_ref, x_ref, g_ref, gate_ref, y_ref, gn_ref, *rest,
                    tm, n_tiles, final, n_split):
    buf, sems = rest[-2:]
    i = pl.program_id(0)
    slot = i % 2

    def issue(idx_ref, s, unroll=4):
        def body(c, carry):
            for u in range(unroll):
                r = c * unroll + u
                for k in range(TOP_K):
                    _row_copy(y_ref, buf.at[s, k], idx_ref[0, 0, TOP_K * r + k], r,
                              sems.at[s]).start(priority=k % 2)
            return carry
        lax.fori_loop(0, tm // unroll, body, 0)

    @pl.when(i == 0)
    def _():
        issue(pos_ref, 0)

    @pl.when(i + 1 < n_tiles)
    def _():
        issue(pos_next_ref, 1 - slot)

    for k in range(TOP_K):
        _wait_rows(tm, y_ref, buf.at[slot, k], sems.at[slot])
    gate = gate_ref[...]
    moe = buf[slot, 0] * gate[:, 0:1] + buf[slot, 1] * gate[:, 1:2]
    xn = x_ref[...] + g_ref[0] * moe
    yn = _rms(xn, gn_ref[...])
    if final:
        ya_ref, yb_ref = rest[:2]

        @pl.when(i < n_split)
        def _():
            ya_ref[...] = yn

        @pl.when(i >= n_split)
        def _():
            yb_ref[...] = yn
    else:
        sc_ref, sh_ref, o_ref, u_ref = rest[:4]
        o_ref[...] = xn
        u_ref[...] = (yn * (1.0 + sc_ref[0]) + sh_ref[0]).astype(u_ref.dtype)


def moe_combine(x, y, pos, gate, mod3, g_chunk, row_fn, g_next, mod3_next=None, n_first=0, tm=128):
    M, D = x.shape
    n_tiles = M // tm
    final = mod3_next is None
    n_split = n_first // tm
    pos3 = pos.reshape(n_tiles, 1, TOP_K * tm)
    row_spec = pl.BlockSpec((tm, D), lambda i: (i, 0))
    in_specs = [
        pl.BlockSpec((1, 1, TOP_K * tm), lambda i: (i, 0, 0), memory_space=pltpu.SMEM),
        pl.BlockSpec((1, 1, TOP_K * tm), lambda i: (jnp.minimum(i + 1, n_tiles - 1), 0, 0),
                     memory_space=pltpu.SMEM),
        row_spec,
        pl.BlockSpec((1, 1, D), lambda i: (row_fn(i), 0, g_chunk)),
        pl.BlockSpec((tm, TOP_K), lambda i: (i, 0)),
        pl.BlockSpec(memory_space=pl.ANY),
        pl.BlockSpec((1, D), lambda i: (0, 0)),
    ]
    args = [pos3, pos3, x, mod3, gate, y, g_next.reshape(1, D)]
    scratch = [pltpu.VMEM((2, TOP_K, tm, D), F32), pltpu.SemaphoreType.DMA((2,))]
    kern = functools.partial(_combine_kernel, tm=tm, n_tiles=n_tiles, final=final, n_split=n_split)
    if final:
        out_specs = [pl.BlockSpec((tm, D), lambda i: (jnp.minimum(i, n_split - 1), 0)),
                     pl.BlockSpec((tm, D), lambda i: (jnp.maximum(i - n_split, 0), 0))]
        return pl.pallas_call(
            kern, grid=(n_tiles,), in_specs=in_specs, out_specs=out_specs,
            out_shape=[jax.ShapeDtypeStruct((n_first, D), F32),
                       jax.ShapeDtypeStruct((M - n_first, D), F32)],
            scratch_shapes=scratch, compiler_params=_cparams(1), name="moe_combine_final")(*args)
    in_specs += [pl.BlockSpec((1, 1, D), lambda i: (row_fn(i), 0, 1)),
                 pl.BlockSpec((1, 1, D), lambda i: (row_fn(i), 0, 0))]
    return pl.pallas_call(
        kern, grid=(n_tiles,), in_specs=in_specs, out_specs=[row_spec, row_spec],
        out_shape=[jax.ShapeDtypeStruct((M, D), F32), jax.ShapeDtypeStruct((M, D), BF16)],
        scratch_shapes=scratch, input_output_aliases={2: 0},
        compiler_params=_cparams(1), name="moe_combine")(*args, mod3_next, mod3_next)


def kernel(x_prompt, x_sample, cache_k, cache_v, state_rglru, c, c_ctx, w_ada, b_ada, norm_attn,
           norm_ffn, norm_final, w_in, w_pool, pool_scale, rpb, conv_w, conv_b, rg_w_a, rg_b_a,
           rg_w_i, rg_b_i, rg_lam, w_out, w_router, b_router, w_up, w_down):
    B, S, D = x_prompt.shape
    DB, T, _ = x_sample.shape
    L = w_in.shape[0]
    pool_w = w_pool.shape[1] * w_pool.shape[2]
    na_w = cache_k.shape[3] * cache_k.shape[4]
    rg_w = rg_lam.shape[-1]
    n_ctx = B * S
    n_tok = n_ctx + DB * T
    q_col0 = pool_w
    x_col0 = pool_w + 3 * na_w

    n_cond = 8
    cond = jnp.concatenate([c_ctx[None, :], c, jnp.zeros((n_cond - 1 - DB, D), F32)], axis=0)
    mod = ada_mod(cond, w_ada, b_ada)

    wr = jnp.zeros((D, 128), F32).at[:, :N_EXPERTS].set(w_router).astype(BF16)
    br = jnp.zeros((1, 128), F32).at[0, :N_EXPERTS].set(b_router)
    n_blocks = (n_tok * TOP_K) // MOE_BLOCK + N_EXPERTS

    ck4 = cache_k.reshape(DB, L, cache_k.shape[2], na_w)
    cv4 = cache_v.reshape(DB, L, cache_v.shape[2], na_w)
    rpb_flat = rpb.reshape(L, -1)
    rg_params = dict(conv_w=conv_w, conv_b=conv_b, rg_w_a=rg_w_a, rg_w_i=rg_w_i, rg_b_a=rg_b_a,
                     rg_b_i=rg_b_i, rg_lam=rg_lam)
    zeros_h = jnp.zeros((B, 2, rg_w), F32)

    new_k, new_v, new_h = [], [], []
    row256 = _cond_row_fn(256, n_ctx, T)
    row512 = _cond_row_fn(512, n_ctx, T)
    row128 = _cond_row_fn(128, n_ctx, T)
    mod3s = [mod[l].reshape(n_cond, 1, 6 * D) for l in range(L)]
    x, u = norm_mod_first(x_prompt.reshape(n_ctx, D), x_sample.reshape(DB * T, D), norm_attn[0],
                          mod3s[0], 1, 0, row256)
    for l in range(L):
        mod3 = mod3s[l]
        proj = mm_ws(u, w_in, l, tm=1024, tn=1024)

        kv = proj[:n_ctx, q_col0 + na_w:q_col0 + 3 * na_w]
        new_k.append(kv[:, :na_w].reshape(B, S, -1, NA_HEAD_DIM))
        new_v.append(kv[:, na_w:].reshape(B, S, -1, NA_HEAD_DIM))

        mix = pool_mixer(proj, w_pool, pool_scale, l, None, S, B, 0, D)
        mix = pool_mixer(proj, w_pool, pool_scale, l, mix, T, DB, n_ctx // T, D)
        mix = ctx_attention(proj, mix, S, B, na_w, q_col0)
        mix = natten(proj, ck4, cv4, rpb_flat, l, mix, T, DB, n_ctx // T, q_col0, na_w)
        mix, h_ctx = rglru(proj, rg_params, l, zeros_h, mix, S, B, 0, x_col0, pool_w + na_w)
        mix, _ = rglru(proj, rg_params, l, state_rglru[:, l], mix, T, DB, n_ctx // T, x_col0,
                       pool_w + na_w)
        new_h.append(h_ctx)

        x = mm_ws(mix, w_out, l, tm=512, tn=1024, res=x, gate=(mod3, 2, row512))

        u2, logits = norm_mod_router(x, norm_ffn[l], mod3, 4, 3, row256, (wr, br))
        slot_tok, gate, block_e, n_active, pos = route(logits[:, :N_EXPERTS], n_blocks)
        xs = gather_rows(u2, slot_tok, n_active, n_blocks)
        hmid = moe_up(xs, w_up, l, block_e, n_active)
        y = moe_down(hmid, w_down, l, block_e, n_active)
        if l + 1 < L:
            x, u = moe_combine(x, y, pos, gate, mod3, 5, row128, norm_attn[l + 1], mod3s[l + 1])
        else:
            y_ctx, y_lat = moe_combine(x, y, pos, gate, mod3, 5, row128, norm_final, n_first=n_ctx)

    return (y_ctx.reshape(B, S, D), y_lat.reshape(DB, T, D), jnp.stack(new_k, axis=1), jnp.stack(new_v, axis=1),
            jnp.stack(new_h, axis=1))
```
